```python
import math
import jax, jax.numpy as jnp
from jax import lax
import numpy as np

D_MODEL = 4096
BATCH = 2
SEQ = 8192
DEPTH = 4
DEC_BATCH = 1
DEC_SEQ = 16384
PAST_LEN = 128

GRID_W = 64
N_MEM = 256
M_HEADS = 4
M_DV = D_MODEL // 2 // M_HEADS
M_DK = M_DV // 2
M_CHUNK = 64
GATE_CAP = 15.0
M_QK_W = M_HEADS * M_DK
M_V_W = M_HEADS * M_DV
N_GATES = 4 * M_HEADS
NA_DH = 128
NA_HEADS = D_MODEL // 2 // NA_DH
NA_W = NA_HEADS * NA_DH
NA_KR = 8
NA_KC = 16
MIX_W = M_V_W + NA_W
N_IN = 2 * M_QK_W + 2 * M_V_W + N_GATES + 3 * NA_W
X_HEADS = 4
X_DH = 128
X_W = X_HEADS * X_DH
D_FF = -(-8 * D_MODEL // (3 * 256)) * 256
RMS_EPS = 1e-6

kernel_name = "hybrid_mlstm_natten_encoder"


def rmsnorm(x, g):
    xf = x.astype(jnp.float32)
    y = xf * lax.rsqrt(jnp.mean(xf * xf, axis=-1, keepdims=True) + RMS_EPS)
    return (y * g.astype(jnp.float32)).astype(x.dtype)


def mlstm_scan(q, k, v, i_pre, f_pre):
    B, H, T, dk = q.shape
    dv = v.shape[-1]
    L = M_CHUNK
    nc = T // L

    def chunks(t):
        return jnp.moveaxis(t.reshape(B, H, nc, L, *t.shape[3:]), 2, 0)

    log_f = jax.nn.log_sigmoid(f_pre)
    lower = jnp.tril(jnp.ones((L, L), dtype=bool))

    def step(carry, inp):
        c_prev, n_prev, m_prev = carry
        qc, kc, vc, li, lf = inp
        b = jnp.cumsum(lf, axis=-1)
        d_log = jnp.where(lower, b[..., :, None] - b[..., None, :] + li[..., None, :], -jnp.inf)
        m_inter = b + m_prev[..., None]
        m_row = jnp.maximum(m_inter, jnp.max(d_log, axis=-1))
        s = jnp.einsum('bhjd,bhsd->bhjs', qc, kc) * jnp.exp(d_log - m_row[..., None])
        w_inter = jnp.exp(m_inter - m_row)
        num = (w_inter[..., None] * jnp.einsum('bhjd,bhde->bhje', qc, c_prev)
               + jnp.einsum('bhjs,bhse->bhje', s, vc))
        den = w_inter * jnp.einsum('bhjd,bhd->bhj', qc, n_prev) + jnp.sum(s, axis=-1)
        h = num / jnp.maximum(jnp.abs(den), jnp.exp(-m_row))[..., None]
        b_last = b[..., -1]
        g = b_last[..., None] - b + li
        m_new = jnp.maximum(b_last + m_prev, jnp.max(g, axis=-1))
        w_tok = jnp.exp(g - m_new[..., None])
        decay = jnp.exp(b_last + m_prev - m_new)
        c_new = decay[..., None, None] * c_prev + jnp.einsum('bhs,bhsd,bhse->bhde', w_tok, kc, vc)
        n_new = decay[..., None] * n_prev + jnp.einsum('bhs,bhsd->bhd', w_tok, kc)
        return (c_new, n_new, m_new), h

    init = (jnp.zeros((B, H, dk, dv), jnp.float32),
            jnp.zeros((B, H, dk), jnp.float32),
            jnp.zeros((B, H), jnp.float32))
    _, hs = lax.scan(step, init, (chunks(q), chunks(k), chunks(v), chunks(i_pre), chunks(log_f)))
    return jnp.moveaxis(hs, 0, 2).reshape(B, H, T, dv)


def mlstm_mixer(q, k, v, o, gates, i_bias, f_bias, g_out):
    B, T, _ = q.shape

    def heads(t, d):
        return t.reshape(B, T, M_HEADS, d).transpose(0, 2, 1, 3).astype(jnp.float32)

    qh = heads(q, M_DK)
    kh = heads(k, M_DK) * (M_DK ** -0.5)
    vh = heads(v, M_DV)
    gt = gates.astype(jnp.float32).reshape(B, T, 4, M_HEADS).transpose(2, 0, 3, 1)
    ib = i_bias.astype(jnp.float32)
    fb = f_bias.astype(jnp.float32)

    def cap(z):
        return GATE_CAP * jnp.tanh(z / GATE_CAP)

    i_fw = cap(gt[0] + ib[0][None, :, None])
    f_fw = cap(gt[1] + fb[0][None, :, None])
    i_bw = cap(gt[2] + ib[1][None, :, None])
    f_bw = cap(gt[3] + fb[1][None, :, None])

    def flip(t):
        return jnp.flip(t, axis=2)

    h_fw = mlstm_scan(qh, kh, vh, i_fw, f_fw)
    h_bw = flip(mlstm_scan(flip(qh), flip(kh), flip(vh), flip(i_bw), flip(f_bw)))
    h = h_fw + h_bw
    h = h * lax.rsqrt(jnp.mean(h * h, axis=-1, keepdims=True) + RMS_EPS)
    h = h * g_out.astype(jnp.float32).reshape(M_HEADS, M_DV)[None, :, None, :]
    h = h.transpose(0, 2, 1, 3).reshape(B, T, M_V_W)
    return (h * jax.nn.sigmoid(o.astype(jnp.float32))).astype(q.dtype)


def neighborhood_attention(q, k, v, rpb):
    B, T, _ = q.shape
    rows = T // GRID_W
    kr = min(NA_KR, rows)

    def grid(t):
        return t.reshape(B, rows, GRID_W, NA_HEADS, NA_DH)

    qg = grid(q) * (NA_DH ** -0.5)
    kg = grid(k)
    vg = grid(v)
    cols = jnp.arange(GRID_W)
    col_idx = jnp.clip(cols - NA_KC // 2, 0, GRID_W - NA_KC)[:, None] + jnp.arange(NA_KC)[None, :]
    col_bias = rpb[:, :, col_idx - cols[:, None] + NA_KC - 1]

    def row_block(r):
        rs = jnp.clip(r - kr // 2, 0, rows - kr)
        k_win = lax.dynamic_slice_in_dim(kg, rs, kr, axis=1)[:, :, col_idx]
        v_win = lax.dynamic_slice_in_dim(vg, rs, kr, axis=1)[:, :, col_idx]
        q_r = lax.dynamic_index_in_dim(qg, r, axis=1, keepdims=False)
        s = jnp.einsum('bqhd,brqchd->bhqrc', q_r, k_win).astype(jnp.float32)
        bias = col_bias[:, rs + jnp.arange(kr) - r + NA_KR - 1]
        s = s + jnp.transpose(bias, (0, 2, 1, 3)).astype(jnp.float32)[None]
        p = jax.nn.softmax(s.reshape(B, NA_HEADS, GRID_W, kr * NA_KC), axis=-1).reshape(s.shape)
        return jnp.einsum('bhqrc,brqchd->bqhd', p.astype(v.dtype), v_win)

    out = lax.map(row_block, jnp.arange(rows))
    return jnp.moveaxis(out, 0, 1).reshape(B, T, NA_W)


def memory_xattn(a, mem_n, wq, wk, wv, wo):
    B, T, _ = a.shape
    M = mem_n.shape[1]
    q = (a @ wq).reshape(B, T, X_HEADS, X_DH)
    k = (mem_n @ wk).reshape(B, M, X_HEADS, X_DH)
    v = (mem_n @ wv).reshape(B, M, X_HEADS, X_DH)
    s = jnp.einsum('bthd,bmhd->bhtm', q, k).astype(jnp.float32) * (X_DH ** -0.5)
    p = jax.nn.softmax(s, axis=-1).astype(a.dtype)
    o = jnp.einsum('bhtm,bmhd->bthd', p, v).reshape(B, T, X_W)
    return o @ wo


def encoder_forward(x, mem, w_in, w_out, g_pre_mix, g_post_mix, i_bias, f_bias, g_mlstm_out, rpb,
                    g_pre_xattn, g_post_xattn, g_mem, wq_x, wk_x, wv_x, wo_x,
                    g_pre_ffn, g_post_ffn, w_gate, w_up, w_down):
    splits = np.cumsum([M_QK_W, M_QK_W, M_V_W, M_V_W, N_GATES, NA_W, NA_W]).tolist()
    h = x
    for l in range(DEPTH):
        a = rmsnorm(h, g_pre_mix[l])
        z = a @ w_in[l]
        q_m, k_m, v_m, o_m, gates, q_n, k_n, v_n = jnp.split(z, splits, axis=-1)
        y_m = mlstm_mixer(q_m, k_m, v_m, o_m, gates, i_bias[l], f_bias[l], g_mlstm_out[l])
        y_n = neighborhood_attention(q_n, k_n, v_n, rpb[l])
        y = jnp.concatenate([y_m, y_n], axis=-1) @ w_out[l]
        h = h + rmsnorm(y, g_post_mix[l])
        a = rmsnorm(h, g_pre_xattn[l])
        y = memory_xattn(a, rmsnorm(mem, g_mem[l]), wq_x[l], wk_x[l], wv_x[l], wo_x[l])
        h = h + rmsnorm(y, g_post_xattn[l])
        a = rmsnorm(h, g_pre_ffn[l])
        y = (jax.nn.silu(a @ w_gate[l]) * (a @ w_up[l])) @ w_down[l]
        h = h + rmsnorm(y, g_post_ffn[l])
    return h


def setup_inputs(seed: int = 0) -> dict:
    key = jax.random.key(seed)
    ks = jax.random.split(key, 24)
    f32 = jnp.float32

    def nrm(k, shape, fan_in):
        return jax.random.normal(k, shape, f32) * (fan_in ** -0.5)

    def gain(k, shape):
        return 1.0 + 0.02 * jax.random.normal(k, shape, f32)

    return {
        "x_prompt": jax.random.normal(ks[0], (BATCH, SEQ, D_MODEL), f32),
        "x_sample": jax.random.normal(ks[1], (DEC_BATCH, DEC_SEQ, D_MODEL), f32),
        "mem_prompt": jax.random.normal(ks[2], (BATCH, N_MEM, D_MODEL), f32),
        "mem_sample": jax.random.normal(ks[3], (DEC_BATCH, N_MEM, D_MODEL), f32),
        "w_in": nrm(ks[4], (DEPTH, D_MODEL, N_IN), D_MODEL),
        "w_out": nrm(ks[5], (DEPTH, MIX_W, D_MODEL), MIX_W),
        "g_pre_mix": gain(ks[6], (DEPTH, D_MODEL)),
        "g_post_mix": gain(ks[7], (DEPTH, D_MODEL)),
        "i_bias": 0.1 * jax.random.normal(ks[8], (DEPTH, 2, M_HEADS), f32),
        "f_bias": 3.0 + 3.0 * jax.random.uniform(ks[9], (DEPTH, 2, M_HEADS), f32),
        "g_mlstm_out": gain(ks[10], (DEPTH, M_V_W)),
        "rpb": 0.02 * jax.random.normal(ks[11], (DEPTH, NA_HEADS, 2 * NA_KR - 1, 2 * NA_KC - 1), f32),
        "g_pre_xattn": gain(ks[12], (DEPTH, D_MODEL)),
        "g_post_xattn": gain(ks[13], (DEPTH, D_MODEL)),
        "g_mem": gain(ks[14], (DEPTH, D_MODEL)),
        "wq_x": nrm(ks[15], (DEPTH, D_MODEL, X_W), D_MODEL),
        "wk_x": nrm(ks[16], (DEPTH, D_MODEL, X_W), D_MODEL),
        "wv_x": nrm(ks[17], (DEPTH, D_MODEL, X_W), D_MODEL),
        "wo_x": nrm(ks[18], (DEPTH, X_W, D_MODEL), X_W),
        "g_pre_ffn": gain(ks[19], (DEPTH, D_MODEL)),
        "g_post_ffn": gain(ks[20], (DEPTH, D_MODEL)),
        "w_gate": nrm(ks[21], (DEPTH, D_MODEL, D_FF), D_MODEL),
        "w_up": nrm(ks[22], (DEPTH, D_MODEL, D_FF), D_MODEL),
        "w_down": nrm(ks[23], (DEPTH, D_FF, D_MODEL), D_FF),
    }


def reference(x_prompt, x_sample, mem_prompt, mem_sample, w_in, w_out, g_pre_mix, g_post_mix,
              i_bias, f_bias, g_mlstm_out, rpb, g_pre_xattn, g_post_xattn, g_mem,
              wq_x, wk_x, wv_x, wo_x, g_pre_ffn, g_post_ffn, w_gate, w_up, w_down):
    y_prompt = encoder_forward(x_prompt, mem_prompt, w_in, w_out, g_pre_mix, g_post_mix, i_bias, f_bias,
                               g_mlstm_out, rpb, g_pre_xattn, g_post_xattn, g_mem, wq_x, wk_x, wv_x, wo_x,
                               g_pre_ffn, g_post_ffn, w_gate, w_up, w_down)
    y_sample = encoder_forward(x_sample, mem_sample, w_in, w_out, g_pre_mix, g_post_mix, i_bias, f_bias,
                               g_mlstm_out, rpb, g_pre_xattn, g_post_xattn, g_mem, wq_x, wk_x, wv_x, wo_x,
                               g_pre_ffn, g_post_ffn, w_gate, w_up, w_down)
    return (y_prompt, y_sample)
```

```python
import functools

import jax
import jax.numpy as jnp
import numpy as np
from jax import lax
from jax.experimental import pallas as pl
from jax.experimental.pallas import tpu as pltpu

GRID_W = 64
M_HEADS = 4
GATE_CAP = 15.0
NA_HEADS = 16
NA_KR = 8
NA_KC = 16
X_HEADS = 4
RMS_EPS = 1e-6
GATE_LANES = 128
MASK_NEG = -1e30

V7X_VMEM_BYTES = 64 * 1024 * 1024
V7X_VMEM_BUDGET = 56 * 1024 * 1024
LANE = 128

MLSTM_CHUNK = 256
TOK_TILE = 1024
ELEM_TILE = 256
XATTN_TILE = 512


def _params(block_bytes, semantics):
    limit = min(V7X_VMEM_BUDGET, int(2 * block_bytes) + 16 * 1024 * 1024)
    return pltpu.CompilerParams(dimension_semantics=semantics, vmem_limit_bytes=limit)


def _nbytes(shape, dtype):
    return int(np.prod(shape)) * jnp.dtype(dtype).itemsize


def _tile(m, pref):
    t = min(m, pref)
    while m % t:
        t //= 2
    return t


def _rms_scale(x):
    return lax.rsqrt(jnp.mean(x * x, axis=-1, keepdims=True) + RMS_EPS)


def _rmsnorm_cast_kernel(x_ref, g_ref, o_ref):
    x = x_ref[...]
    o_ref[...] = (x * _rms_scale(x) * g_ref[...]).astype(o_ref.dtype)


def rmsnorm_cast(x, g):
    m, d = x.shape
    tm = _tile(m, ELEM_TILE)
    blk = _nbytes((tm, d), jnp.float32) + _nbytes((tm, d), jnp.bfloat16)
    return pl.pallas_call(
        _rmsnorm_cast_kernel,
        out_shape=jax.ShapeDtypeStruct((m, d), jnp.bfloat16),
        grid=(m // tm,),
        in_specs=[pl.BlockSpec((tm, d), lambda i: (i, 0)),
                  pl.BlockSpec((1, d), lambda i: (0, 0))],
        out_specs=pl.BlockSpec((tm, d), lambda i: (i, 0)),
        compiler_params=_params(blk, ("parallel",)),
        name="rmsnorm_cast",
    )(x, g.reshape(1, d))


def _residual_norm_kernel(h_ref, y_ref, gpost_ref, gpre_ref, h_out_ref, a_out_ref):
    y = y_ref[...]
    h = h_ref[...] + y * _rms_scale(y) * gpost_ref[...]
    h_out_ref[...] = h
    a_out_ref[...] = (h * _rms_scale(h) * gpre_ref[...]).astype(a_out_ref.dtype)


def _residual_kernel(h_ref, y_ref, gpost_ref, h_out_ref):
    y = y_ref[...]
    h_out_ref[...] = h_ref[...] + y * _rms_scale(y) * gpost_ref[...]


def residual_norm(h, y, g_post, g_pre_next):
    m, d = h.shape
    tm = _tile(m, ELEM_TILE)
    row = pl.BlockSpec((tm, d), lambda i: (i, 0))
    vec = pl.BlockSpec((1, d), lambda i: (0, 0))
    blk = 3 * _nbytes((tm, d), jnp.float32) + _nbytes((tm, d), jnp.bfloat16)
    if g_pre_next is None:
        return pl.pallas_call(
            _residual_kernel,
            out_shape=jax.ShapeDtypeStruct((m, d), jnp.float32),
            grid=(m // tm,), in_specs=[row, row, vec], out_specs=row,
            compiler_params=_params(blk, ("parallel",)),
            name="residual",
        )(h, y, g_post.reshape(1, d)), None
    return pl.pallas_call(
        _residual_norm_kernel,
        out_shape=(jax.ShapeDtypeStruct((m, d), jnp.float32),
                   jax.ShapeDtypeStruct((m, d), jnp.bfloat16)),
        grid=(m // tm,), in_specs=[row, row, vec, vec], out_specs=(row, row),
        compiler_params=_params(blk, ("parallel",)),
        name="residual_norm",
    )(h, y, g_post.reshape(1, d), g_pre_next.reshape(1, d))


def _matmul_kernel(*refs, n_pairs):
    o_ref = refs[-1]
    acc = jnp.dot(refs[0][...], refs[n_pairs][...], preferred_element_type=jnp.float32)
    for p in range(1, n_pairs):
        acc += jnp.dot(refs[p][...], refs[n_pairs + p][...], preferred_element_type=jnp.float32)
    o_ref[...] = acc.astype(o_ref.dtype)


def matmul(a_list, w_list, out_dtype, tn_pref=1024, tm_pref=TOK_TILE):
    m = a_list[0].shape[0]
    n = w_list[0].shape[1]
    tm = _tile(m, tm_pref)
    tn = _tile(n, tn_pref)
    in_specs, blk = [], 0
    for a in a_list:
        k = a.shape[1]
        in_specs.append(pl.BlockSpec((tm, k), lambda i, j: (i, 0)))
        blk += _nbytes((tm, k), a.dtype)
    for w in w_list:
        k = w.shape[0]
        in_specs.append(pl.BlockSpec((k, tn), lambda i, j: (0, j)))
        blk += _nbytes((k, tn), w.dtype)
    blk += _nbytes((tm, tn), out_dtype) + _nbytes((tm, tn), jnp.float32)
    return pl.pallas_call(
        functools.partial(_matmul_kernel, n_pairs=len(a_list)),
        out_shape=jax.ShapeDtypeStruct((m, n), out_dtype),
        grid=(m // tm, n // tn),
        in_specs=in_specs,
        out_specs=pl.BlockSpec((tm, tn), lambda i, j: (i, j)),
        compiler_params=_params(blk, ("parallel", "arbitrary")),
        name="matmul",
    )(*a_list, *w_list)


def _gate_up_kernel(a_ref, wg_ref, wu_ref, o_ref):
    a = a_ref[...]
    g = jnp.dot(a, wg_ref[...], preferred_element_type=jnp.float32)
    u = jnp.dot(a, wu_ref[...], preferred_element_type=jnp.float32)
    o_ref[...] = (g * jax.nn.sigmoid(g) * u).astype(o_ref.dtype)


def ffn_gate_up(a, wg, wu):
    m, k = a.shape
    n = wg.shape[1]
    tm = _tile(m, TOK_TILE)
    tn = _tile(n, 256)
    blk = (_nbytes((tm, k), a.dtype) + 2 * _nbytes((k, tn), wg.dtype)
           + _nbytes((tm, tn), jnp.bfloat16) + 3 * _nbytes((tm, tn), jnp.float32))
    return pl.pallas_call(
        _gate_up_kernel,
        out_shape=jax.ShapeDtypeStruct((m, n), jnp.bfloat16),
        grid=(m // tm, n // tn),
        in_specs=[pl.BlockSpec((tm, k), lambda i, j: (i, 0)),
                  pl.BlockSpec((k, tn), lambda i, j: (0, j)),
                  pl.BlockSpec((k, tn), lambda i, j: (0, j))],
        out_specs=pl.BlockSpec((tm, tn), lambda i, j: (i, j)),
        compiler_params=_params(blk, ("parallel", "arbitrary")),
        name="ffn_gate_up",
    )(a, wg, wu)


def _ksplit_kernel(a_ref, w_ref, o_ref):
    part = jnp.dot(a_ref[...], w_ref[...], preferred_element_type=jnp.float32)

    @pl.when(pl.program_id(2) == 0)
    def _():
        o_ref[...] = part

    @pl.when(pl.program_id(2) != 0)
    def _():
        o_ref[...] += part


def matmul_ksplit(a, w, k_splits, tn_pref=512):
    m, k = a.shape
    n = w.shape[1]
    tm = _tile(m, TOK_TILE)
    tn = _tile(n, tn_pref)
    tk = k // k_splits
    assert tk * k_splits == k and tk % LANE == 0
    blk = (_nbytes((tm, tk), a.dtype) + _nbytes((tk, tn), w.dtype)
           + 2 * _nbytes((tm, tn), jnp.float32))
    return pl.pallas_call(
        _ksplit_kernel,
        out_shape=jax.ShapeDtypeStruct((m, n), jnp.float32),
        grid=(m // tm, n // tn, k_splits),
        in_specs=[pl.BlockSpec((tm, tk), lambda i, j, s: (i, s)),
                  pl.BlockSpec((tk, tn), lambda i, j, s: (s, j))],
        out_specs=pl.BlockSpec((tm, tn), lambda i, j, s: (i, j)),
        compiler_params=_params(blk, ("parallel", "arbitrary", "arbitrary")),
        name="matmul_ksplit",
    )(a, w)


def _log_sigmoid(x):
    return jnp.minimum(x, 0.0) - jnp.log(1.0 + jnp.exp(-jnp.abs(x)))


def _split3_dot(tri, x):
    hi = x.astype(jnp.bfloat16)
    r1 = x - hi.astype(jnp.float32)
    mid = r1.astype(jnp.bfloat16)
    lo = (r1 - mid.astype(jnp.float32)).astype(jnp.bfloat16)
    out = jnp.dot(tri, hi, preferred_element_type=jnp.float32)
    out += jnp.dot(tri, mid, preferred_element_type=jnp.float32)
    out += jnp.dot(tri, lo, preferred_element_type=jnp.float32)
    return out


def _mlstm_kernel(q_ref, k_ref, v_ref, g_ref, gb_ref, o_ref, c_scr, n_scr, m_scr, *,
                  reverse, n_chunks, seq_first_chunks, dk, dv):
    L = q_ref.shape[0]
    step = pl.program_id(0)
    chunk = (n_chunks - 1 - step) if reverse else step

    is_start = functools.reduce(jnp.logical_or, [chunk == c for c in seq_first_chunks])

    @pl.when(is_start)
    def _():
        c_scr[...] = jnp.zeros_like(c_scr)
        n_scr[...] = jnp.zeros_like(n_scr)
        m_scr[...] = jnp.zeros_like(m_scr)

    rows = lax.broadcasted_iota(jnp.int32, (L, L), 0)
    cols = lax.broadcasted_iota(jnp.int32, (L, L), 1)
    keep = (cols >= rows) if reverse else (cols <= rows)
    tri = keep.astype(jnp.bfloat16)

    pre = g_ref[...] + gb_ref[...]
    capped = GATE_CAP * jnp.tanh(pre / GATE_CAP)
    log_f = _log_sigmoid(capped)
    cum = _split3_dot(tri, log_f)
    capped_t = capped.T
    cum_t = cum.T
    edge = 0 if reverse else L - 1

    d = 1 if reverse else 0
    for h in range(M_HEADS):
        il = (2 * d) * M_HEADS + h
        fl = (2 * d + 1) * M_HEADS + h
        b_col = cum[:, fl:fl + 1]
        li_col = capped[:, il:il + 1]
        total = cum[edge:edge + 1, fl:fl + 1]
        src_row = capped_t[il:il + 1, :] - cum_t[fl:fl + 1, :]
        m_prev = m_scr[h, 0:1, 0:1]

        d_log = jnp.where(keep, b_col + src_row, MASK_NEG)
        m_inter = b_col + m_prev
        m_row = jnp.maximum(m_inter, jnp.max(d_log, axis=1, keepdims=True))

        q = q_ref[:, h * dk:(h + 1) * dk]
        k = k_ref[:, h * dk:(h + 1) * dk] * jnp.asarray(dk ** -0.5, k_ref.dtype)
        v = v_ref[:, h * dv:(h + 1) * dv]
        s = lax.dot_general(q, k, (((1,), (1,)), ((), ())), preferred_element_type=jnp.float32)
        p = s * jnp.exp(d_log - m_row)
        w_inter = jnp.exp(m_inter - m_row)

        c_prev = c_scr[h]
        n_prev = n_scr[h, 0:1, :]
        num = (w_inter * jnp.dot(q, c_prev.astype(jnp.bfloat16), preferred_element_type=jnp.float32)
               + jnp.dot(p.astype(jnp.bfloat16), v, preferred_element_type=jnp.float32))
        den = (w_inter * jnp.sum(q.astype(jnp.float32) * n_prev, axis=1, keepdims=True)
               + jnp.sum(p, axis=1, keepdims=True))
        o_ref[:, h * dv:(h + 1) * dv] = num / jnp.maximum(jnp.abs(den), jnp.exp(-m_row))

        g_col = total - b_col + li_col
        m_new = jnp.maximum(total + m_prev, jnp.max(g_col, axis=0, keepdims=True))
        w_tok = jnp.exp(g_col - m_new)
        decay = jnp.exp(total + m_prev - m_new)
        kw = k.astype(jnp.float32) * w_tok
        c_scr[h] = decay * c_prev + lax.dot_general(
            kw.astype(jnp.bfloat16), v, (((0,), (0,)), ((), ())), preferred_element_type=jnp.float32)
        n_scr[h] = jnp.broadcast_to(decay * n_prev + jnp.sum(kw, axis=0, keepdims=True), n_scr.shape[1:])
        m_scr[h] = jnp.broadcast_to(m_new, m_scr.shape[1:])


def mlstm_scan(z, gates, gate_bias, seqs, reverse, dk, dv):
    m = z.shape[0]
    L = MLSTM_CHUNK
    qw, vw = M_HEADS * dk, M_HEADS * dv
    n_chunks = m // L
    for s0, t in seqs:
        assert s0 % L == 0 and t % L == 0
    if reverse:
        firsts = tuple((s0 + t) // L - 1 for s0, t in seqs)
    else:
        firsts = tuple(s0 // L for s0, _ in seqs)
    cidx = (lambda i: n_chunks - 1 - i) if reverse else (lambda i: i)
    assert qw % vw == 0 or vw % qw == 0
    blk = (2 * _nbytes((L, qw), z.dtype) + _nbytes((L, vw), z.dtype) + _nbytes((L, GATE_LANES), jnp.float32)
           + _nbytes((L, vw), jnp.float32))
    scratch = [pltpu.VMEM((M_HEADS, dk, dv), jnp.float32),
               pltpu.VMEM((M_HEADS, 8, dk), jnp.float32),
               pltpu.VMEM((M_HEADS, 8, LANE), jnp.float32)]
    blk += _nbytes((M_HEADS, dk, dv), jnp.float32)
    return pl.pallas_call(
        functools.partial(_mlstm_kernel, reverse=reverse, n_chunks=n_chunks,
                          seq_first_chunks=firsts, dk=dk, dv=dv),
        out_shape=jax.ShapeDtypeStruct((m, vw), jnp.float32),
        grid=(n_chunks,),
        in_specs=[pl.BlockSpec((L, qw), lambda i: (cidx(i), 0)),
                  pl.BlockSpec((L, qw), lambda i: (cidx(i), 1)),
                  pl.BlockSpec((L, vw), lambda i: (cidx(i), (2 * qw) // vw)),
                  pl.BlockSpec((L, GATE_LANES), lambda i: (cidx(i), 0)),
                  pl.BlockSpec((1, GATE_LANES), lambda i: (0, 0))],
        out_specs=pl.BlockSpec((L, vw), lambda i: (cidx(i), 0)),
        scratch_shapes=scratch,
        compiler_params=_params(blk, ("arbitrary",)),
        name="mlstm_bwd" if reverse else "mlstm_fwd",
    )(z, z, z, gates, gate_bias)


def _mlstm_out_kernel(hf_ref, hb_ref, o_ref, g_ref, y_ref, *, dv):
    for h in range(M_HEADS):
        sl = slice(h * dv, (h + 1) * dv)
        x = hf_ref[:, sl] + hb_ref[:, sl]
        x = x * _rms_scale(x) * g_ref[:, sl]
        y_ref[:, sl] = (x * jax.nn.sigmoid(o_ref[:, sl].astype(jnp.float32))).astype(y_ref.dtype)


def mlstm_out(h_fw, h_bw, z, g_out, o_col_block, dv):
    m, vw = h_fw.shape
    tm = _tile(m, ELEM_TILE)
    blk = 2 * _nbytes((tm, vw), jnp.float32) + 2 * _nbytes((tm, vw), jnp.bfloat16)
    return pl.pallas_call(
        functools.partial(_mlstm_out_kernel, dv=dv),
        out_shape=jax.ShapeDtypeStruct((m, vw), jnp.bfloat16),
        grid=(m // tm,),
        in_specs=[pl.BlockSpec((tm, vw), lambda i: (i, 0)),
                  pl.BlockSpec((tm, vw), lambda i: (i, 0)),
                  pl.BlockSpec((tm, vw), lambda i: (i, o_col_block)),
                  pl.BlockSpec((1, vw), lambda i: (0, 0))],
        out_specs=pl.BlockSpec((tm, vw), lambda i: (i, 0)),
        compiler_params=_params(blk, ("parallel",)),
        name="mlstm_out",
    )(h_fw, h_bw, z, g_out.reshape(1, vw))


def _na_kernel(*refs, dh):
    q_ref = refs[0]
    k_refs = refs[1:1 + NA_KR]
    v_refs = refs[1 + NA_KR:1 + 2 * NA_KR]
    bias_ref = refs[1 + 2 * NA_KR]
    o_ref = refs[2 + 2 * NA_KR]
    scale = dh ** -0.5
    for h in range(NA_HEADS):
        sl = slice(h * dh, (h + 1) * dh)
        q = q_ref[:, sl]
        k = jnp.concatenate([r[0, :, sl] for r in k_refs], axis=0)
        v = jnp.concatenate([r[0, :, sl] for r in v_refs], axis=0)
        s = lax.dot_general(q, k, (((1,), (1,)), ((), ())), preferred_element_type=jnp.float32)
        s = s * scale + bias_ref[0, h]
        s = s - jnp.max(s, axis=1, keepdims=True)
        e = jnp.exp(s)
        p = e / jnp.sum(e, axis=1, keepdims=True)
        o_ref[:, sl] = jnp.dot(p.astype(v.dtype), v, preferred_element_type=jnp.float32).astype(o_ref.dtype)


def _na_dense_bias(rpb_l):
    qs = np.arange(GRID_W)
    cs = np.clip(qs - NA_KC // 2, 0, GRID_W - NA_KC)
    cidx = np.arange(GRID_W)[None, :] - qs[:, None] + NA_KC - 1
    allowed = (np.arange(GRID_W)[None, :] >= cs[:, None]) & (np.arange(GRID_W)[None, :] < cs[:, None] + NA_KC)
    cidx = np.where(allowed, cidx, 0)
    ridx = np.arange(NA_KR)[None, :] - np.arange(NA_KR)[:, None] + NA_KR - 1
    t = rpb_l[:, ridx[:, :, None, None], cidx[None, None, :, :]]
    t = jnp.where(allowed[None, None, None], t, MASK_NEG)
    t = jnp.transpose(t, (1, 0, 3, 2, 4))
    return t.reshape(NA_KR, NA_HEADS, GRID_W, NA_KR * GRID_W).astype(jnp.float32)


def neighborhood_attention(z, bias, seqs, q_col_block, dh):
    m = z.shape[0]
    hw = NA_HEADS * dh
    n_rows = m // GRID_W
    z3 = z.reshape(n_rows, GRID_W, z.shape[1])
    bounds = [(s0 // GRID_W, t // GRID_W) for s0, t in seqs]
    for _, nr in bounds:
        assert nr >= NA_KR

    def win_start(r):
        lo = jnp.int32(0)
        hi = jnp.int32(0)
        for r0, nr in bounds:
            inside = (r >= r0) & (r < r0 + nr)
            lo = jnp.where(inside, r0, lo)
            hi = jnp.where(inside, r0 + nr - NA_KR, hi)
        return jnp.clip(r - NA_KR // 2, lo, hi)

    def kv_spec(i, col_block):
        return pl.BlockSpec((1, GRID_W, hw), lambda r: (win_start(r) + i, 0, col_block))

    in_specs = [pl.BlockSpec((GRID_W, hw), lambda r: (r, q_col_block))]
    in_specs += [kv_spec(i, q_col_block + 1) for i in range(NA_KR)]
    in_specs += [kv_spec(i, q_col_block + 2) for i in range(NA_KR)]
    in_specs.append(pl.BlockSpec((1, NA_HEADS, GRID_W, NA_KR * GRID_W),
                                 lambda r: (r - win_start(r), 0, 0, 0)))
    blk = ((2 + 2 * NA_KR) * _nbytes((GRID_W, hw), z.dtype)
           + _nbytes((NA_HEADS, GRID_W, NA_KR * GRID_W), jnp.float32))
    return pl.pallas_call(
        functools.partial(_na_kernel, dh=dh),
        out_shape=jax.ShapeDtypeStruct((m, hw), jnp.bfloat16),
        grid=(n_rows,),
        in_specs=in_specs,
        out_specs=pl.BlockSpec((GRID_W, hw), lambda r: (r, 0)),
        compiler_params=_params(blk, ("parallel",)),
        name="neighborhood_attention",
    )(z, *([z3] * (2 * NA_KR)), bias)


def _xattn_kernel(q_ref, k_ref, v_ref, o_ref, *, dh):
    scale = dh ** -0.5
    for h in range(X_HEADS):
        sl = slice(h * dh, (h + 1) * dh)
        s = lax.dot_general(q_ref[:, sl], k_ref[:, sl], (((1,), (1,)), ((), ())),
                            preferred_element_type=jnp.float32) * scale
        s = s - jnp.max(s, axis=1, keepdims=True)
        e = jnp.exp(s)
        p = e / jnp.sum(e, axis=1, keepdims=True)
        o_ref[:, sl] = jnp.dot(p.astype(v_ref.dtype), v_ref[:, sl],
                               preferred_element_type=jnp.float32).astype(o_ref.dtype)


def memory_attention(q, k, v, seqs, n_mem):
    m, xw = q.shape
    dh = xw // X_HEADS
    tm = XATTN_TILE
    for s0, t in seqs:
        tm = min(tm, _tile(t, tm))
    starts = [s0 // tm for s0, _ in seqs]
    for s0, _ in seqs:
        assert s0 % tm == 0

    def seq_of(i):
        s = jnp.int32(0)
        for idx, st in enumerate(starts):
            s = jnp.where(i >= st, idx, s)
        return s

    blk = 2 * _nbytes((tm, xw), q.dtype) + 2 * _nbytes((n_mem, xw), k.dtype) + 2 * _nbytes((tm, n_mem), jnp.float32)
    return pl.pallas_call(
        functools.partial(_xattn_kernel, dh=dh),
        out_shape=jax.ShapeDtypeStruct((m, xw), jnp.bfloat16),
        grid=(m // tm,),
        in_specs=[pl.BlockSpec((tm, xw), lambda i: (i, 0)),
                  pl.BlockSpec((n_mem, xw), lambda i: (seq_of(i), 0)),
                  pl.BlockSpec((n_mem, xw), lambda i: (seq_of(i), 0))],
        out_specs=pl.BlockSpec((tm, xw), lambda i: (i, 0)),
        compiler_params=_params(blk, ("parallel",)),
        name="memory_attention",
    )(q, k, v)


def kernel(x_prompt, x_sample, mem_prompt, mem_sample, w_in, w_out, g_pre_mix, g_post_mix, i_bias, f_bias, g_mlstm_out, rpb, g_pre_xattn, g_post_xattn, g_mem, wq_x, wk_x, wv_x, wo_x, g_pre_ffn, g_post_ffn, w_gate, w_up, w_down):
    bf16 = jnp.bfloat16
    depth = w_in.shape[0]
    d = x_prompt.shape[-1]
    bp, tp, _ = x_prompt.shape
    bs, ts, _ = x_sample.shape
    n_mem = mem_prompt.shape[1]
    mix_w = w_out.shape[1]
    v_w = mix_w // 2
    dv = v_w // M_HEADS
    dk = dv // 2
    qk_w = M_HEADS * dk
    na_dh = v_w // NA_HEADS
    n_gates = 4 * M_HEADS
    d_ff = w_gate.shape[2]
    assert w_in.shape[2] == 2 * qk_w + 2 * v_w + n_gates + 3 * v_w

    seqs = [(b * tp, tp) for b in range(bp)] + [(bp * tp + b * ts, ts) for b in range(bs)]
    h = jnp.concatenate([x_prompt.reshape(bp * tp, d), x_sample.reshape(bs * ts, d)], axis=0)
    mem = jnp.concatenate([mem_prompt.reshape(bp * n_mem, d), mem_sample.reshape(bs * n_mem, d)], axis=0)

    m_cols = 2 * qk_w + 2 * v_w
    w_main = jnp.concatenate([w_in[:, :, :m_cols], w_in[:, :, m_cols + n_gates:]], axis=2).astype(bf16)
    w_gates = jnp.pad(w_in[:, :, m_cols:m_cols + n_gates], ((0, 0), (0, 0), (0, GATE_LANES - n_gates))).astype(bf16)
    gate_bias = jnp.concatenate([i_bias[:, 0], f_bias[:, 0], i_bias[:, 1], f_bias[:, 1]], axis=1)
    gate_bias = jnp.pad(gate_bias, ((0, 0), (0, GATE_LANES - n_gates))).astype(jnp.float32)
    w_out_m = w_out[:, :v_w].astype(bf16)
    w_out_n = w_out[:, v_w:].astype(bf16)
    wq, wk, wv, wo = (w.astype(bf16) for w in (wq_x, wk_x, wv_x, wo_x))
    wg, wu, wd = (w.astype(bf16) for w in (w_gate, w_up, w_down))
    ff_splits = 2 if (d_ff // 2) % LANE == 0 and d_ff > 4096 else 1

    a = rmsnorm_cast(h, g_pre_mix[0])
    for l in range(depth):
        z = matmul([a], [w_main[l]], bf16)
        gates = matmul([a], [w_gates[l]], jnp.float32, tn_pref=GATE_LANES)
        gb = gate_bias[l].reshape(1, GATE_LANES)
        h_fw = mlstm_scan(z, gates, gb, seqs, False, dk, dv)
        h_bw = mlstm_scan(z, gates, gb, seqs, True, dk, dv)
        y_m = mlstm_out(h_fw, h_bw, z, g_mlstm_out[l], (2 * qk_w + v_w) // v_w, dv)
        y_n = neighborhood_attention(z, _na_dense_bias(rpb[l]), seqs, m_cols // v_w, na_dh)
        y = matmul([y_m, y_n], [w_out_m[l], w_out_n[l]], jnp.float32)
        h, a = residual_norm(h, y, g_post_mix[l], g_pre_xattn[l])
        mem_n = rmsnorm_cast(mem, g_mem[l])
        q = matmul([a], [wq[l]], bf16)
        k = matmul([mem_n], [wk[l]], bf16)
        v = matmul([mem_n], [wv[l]], bf16)
        o = memory_attention(q, k, v, seqs, n_mem)
        y = matmul([o], [wo[l]], jnp.float32)
        h, a = residual_norm(h, y, g_post_xattn[l], g_pre_ffn[l])
        hidden = ffn_gate_up(a, wg[l], wu[l])
        y = matmul_ksplit(hidden, wd[l], ff_splits)
        h, a = residual_norm(h, y, g_post_ffn[l], g_pre_mix[l + 1] if l + 1 < depth else None)

    return (h[:bp * tp].reshape(bp, tp, d), h[bp * tp:].reshape(bs, ts, d))
```

```python
import functools

import jax
import jax.numpy as jnp
import numpy as np
from jax import lax
from jax.experimental import pallas as pl
from jax.experimental.pallas import tpu as pltpu

GRID_W = 64
M_HEADS = 4
GATE_CAP = 15.0
NA_HEADS = 16
NA_KR = 8
NA_KC = 16
X_HEADS = 4
RMS_EPS = 1e-6
GATE_LANES = 128
MASK_NEG = -1e30

V7X_VMEM_BYTES = 64 * 1024 * 1024
V7X_VMEM_BUDGET = 56 * 1024 * 1024
LANE = 128

MLSTM_CHUNK = 256
TOK_TILE = 1024
ELEM_TILE = 256
XATTN_TILE = 512


def _params(block_bytes, semantics):
    limit = min(V7X_VMEM_BUDGET, int(2 * block_bytes) + 16 * 1024 * 1024)
    return pltpu.CompilerParams(dimension_semantics=semantics, vmem_limit_bytes=limit)


def _nbytes(shape, dtype):
    return int(np.prod(shape)) * jnp.dtype(dtype).itemsize


def _tile(m, pref):
    t = min(m, pref)
    while m % t:
        t //= 2
    return t


def _rms_scale(x):
    return lax.rsqrt(jnp.mean(x * x, axis=-1, keepdims=True) + RMS_EPS)


def _rmsnorm_cast_kernel(x_ref, g_ref, o_ref):
    x = x_ref[...]
    o_ref[...] = (x * _rms_scale(x) * g_ref[...]).astype(o_ref.dtype)


def rmsnorm_cast(x, g):
    m, d = x.shape
    tm = _tile(m, ELEM_TILE)
    blk = _nbytes((tm, d), jnp.float32) + _nbytes((tm, d), jnp.bfloat16)
    return pl.pallas_call(
        _rmsnorm_cast_kernel,
        out_shape=jax.ShapeDtypeStruct((m, d), jnp.bfloat16),
        grid=(m // tm,),
        in_specs=[pl.BlockSpec((tm, d), lambda i: (i, 0)),
                  pl.BlockSpec((1, d), lambda i: (0, 0))],
        out_specs=pl.BlockSpec((tm, d), lambda i: (i, 0)),
        compiler_params=_params(blk, ("parallel",)),
        name="rmsnorm_cast",
    )(x, g.reshape(1, d))


def _residual_norm_kernel(h_ref, y_ref, gpost_ref, gpre_ref, h_out_ref, a_out_ref):
    y = y_ref[...]
    h = h_ref[...] + y * _rms_scale(y) * gpost_ref[...]
    h_out_ref[...] = h
    a_out_ref[...] = (h * _rms_scale(h) * gpre_ref[...]).astype(a_out_ref.dtype)


def _residual_kernel(h_ref, y_ref, gpost_ref, h_out_ref):
    y = y_ref[...]
    h_out_ref[...] = h_ref[...] + y * _rms_scale(y) * gpost_ref[...]


def residual_norm(h, y, g_post, g_pre_next):
    m, d = h.shape
    tm = _tile(m, ELEM_TILE)
    row = pl.BlockSpec((tm, d), lambda i: (i, 0))
    vec = pl.BlockSpec((1, d), lambda i: (0, 0))
    blk = 3 * _nbytes((tm, d), jnp.float32) + _nbytes((tm, d), jnp.bfloat16)
    if g_pre_next is None:
        return pl.pallas_call(
            _residual_kernel,
            out_shape=jax.ShapeDtypeStruct((m, d), jnp.float32),
            grid=(m // tm,), in_specs=[row, row, vec], out_specs=row,
            compiler_params=_params(blk, ("parallel",)),
            name="residual",
        )(h, y, g_post.reshape(1, d)), None
    return pl.pallas_call(
        _residual_norm_kernel,
        out_shape=(jax.ShapeDtypeStruct((m, d), jnp.float32),
                   jax.ShapeDtypeStruct((m, d), jnp.bfloat16)),
        grid=(m // tm,), in_specs=[row, row, vec, vec], out_specs=(row, row),
        compiler_params=_params(blk, ("parallel",)),
        name="residual_norm",
    )(h, y, g_post.reshape(1, d), g_pre_next.reshape(1, d))


def _matmul_kernel(*refs, n_pairs):
    o_ref = refs[-1]
    acc = jnp.dot(refs[0][...], refs[n_pairs][...], preferred_element_type=jnp.float32)
    for p in range(1, n_pairs):
        acc += jnp.dot(refs[p][...], refs[n_pairs + p][...], preferred_element_type=jnp.float32)
    o_ref[...] = acc.astype(o_ref.dtype)


def matmul(a_list, w_list, out_dtype, tn_pref=1024, tm_pref=TOK_TILE):
    m = a_list[0].shape[0]
    n = w_list[0].shape[1]
    tm = _tile(m, tm_pref)
    tn = _tile(n, tn_pref)
    in_specs, blk = [], 0
    for a in a_list:
        k = a.shape[1]
        in_specs.append(pl.BlockSpec((tm, k), lambda i, j: (i, 0)))
        blk += _nbytes((tm, k), a.dtype)
    for w in w_list:
        k = w.shape[0]
        in_specs.append(pl.BlockSpec((k, tn), lambda i, j: (0, j)))
        blk += _nbytes((k, tn), w.dtype)
    blk += _nbytes((tm, tn), out_dtype) + _nbytes((tm, tn), jnp.float32)
    return pl.pallas_call(
        functools.partial(_matmul_kernel, n_pairs=len(a_list)),
        out_shape=jax.ShapeDtypeStruct((m, n), out_dtype),
        grid=(m // tm, n // tn),
        in_specs=in_specs,
        out_specs=pl.BlockSpec((tm, tn), lambda i, j: (i, j)),
        compiler_params=_params(blk, ("parallel", "arbitrary")),
        name="matmul",
    )(*a_list, *w_list)


def _gate_up_kernel(a_ref, wg_ref, wu_ref, o_ref):
    a = a_ref[...]
    g = jnp.dot(a, wg_ref[...], preferred_element_type=jnp.float32)
    u = jnp.dot(a, wu_ref[...], preferred_element_type=jnp.float32)
    o_ref[...] = (g * jax.nn.sigmoid(g) * u).astype(o_ref.dtype)


def ffn_gate_up(a, wg, wu):
    m, k = a.shape
    n = wg.shape[1]
    tm = _tile(m, TOK_TILE)
    tn = _tile(n, 256)
    blk = (_nbytes((tm, k), a.dtype) + 2 * _nbytes((k, tn), wg.dtype)
           + _nbytes((tm, tn), jnp.bfloat16) + 3 * _nbytes((tm, tn), jnp.float32))
    return pl.pallas_call(
        _gate_up_kernel,
        out_shape=jax.ShapeDtypeStruct((m, n), jnp.bfloat16),
        grid=(m // tm, n // tn),
        in_specs=[pl.BlockSpec((tm, k), lambda i, j: (i, 0)),
                  pl.BlockSpec((k, tn), lambda i, j: (0, j)),
                  pl.BlockSpec((k, tn), lambda i, j: (0, j))],
        out_specs=pl.BlockSpec((tm, tn), lambda i, j: (i, j)),
        compiler_params=_params(blk, ("parallel", "arbitrary")),
        name="ffn_gate_up",
    )(a, wg, wu)


def _ksplit_kernel(a_ref, w_ref, o_ref):
    part = jnp.dot(a_ref[...], w_ref[...], preferred_element_type=jnp.float32)

    @pl.when(pl.program_id(2) == 0)
    def _():
        o_ref[...] = part

    @pl.when(pl.program_id(2) != 0)
    def _():
        o_ref[...] += part


def matmul_ksplit(a, w, k_splits, tn_pref=512):
    m, k = a.shape
    n = w.shape[1]
    tm = _tile(m, TOK_TILE)
    tn = _tile(n, tn_pref)
    tk = k // k_splits
    assert tk * k_splits == k and tk % LANE == 0
    blk = (_nbytes((tm, tk), a.dtype) + _nbytes((tk, tn), w.dtype)
           + 2 * _nbytes((tm, tn), jnp.float32))
    return pl.pallas_call(
        _ksplit_kernel,
        out_shape=jax.ShapeDtypeStruct((m, n), jnp.float32),
        grid=(m // tm, n // tn, k_splits),
        in_specs=[pl.BlockSpec((tm, tk), lambda i, j, s: (i, s)),
                  pl.BlockSpec((tk, tn), lambda i, j, s: (s, j))],
        out_specs=pl.BlockSpec((tm, tn), lambda i, j, s: (i, j)),
        compiler_params=_params(blk, ("parallel", "arbitrary", "arbitrary")),
        name="matmul_ksplit",
    )(a, w)


def _log_sigmoid(x):
    return jnp.minimum(x, 0.0) - jnp.log(1.0 + jnp.exp(-jnp.abs(x)))


def _split3_dot(tri, x):
    hi = x.astype(jnp.bfloat16)
    r1 = x - hi.astype(jnp.float32)
    mid = r1.astype(jnp.bfloat16)
    lo = (r1 - mid.astype(jnp.float32)).astype(jnp.bfloat16)
    out = jnp.dot(tri, hi, preferred_element_type=jnp.float32)
    out += jnp.dot(tri, mid, preferred_element_type=jnp.float32)
    out += jnp.dot(tri, lo, preferred_element_type=jnp.float32)
    return out


def _mlstm_kernel(*refs, reverse, n_chunks, seq_first_chunks, dk, dv, combine):
    if combine:
        q_ref, k_ref, v_ref, g_ref, gb_ref, hf_ref, og_ref, gout_ref, o_ref, c_scr, n_scr, m_scr = refs
    else:
        q_ref, k_ref, v_ref, g_ref, gb_ref, o_ref, c_scr, n_scr, m_scr = refs
    L = q_ref.shape[0]
    step = pl.program_id(0)
    chunk = (n_chunks - 1 - step) if reverse else step

    is_start = functools.reduce(jnp.logical_or, [chunk == c for c in seq_first_chunks])

    @pl.when(is_start)
    def _():
        c_scr[...] = jnp.zeros_like(c_scr)
        n_scr[...] = jnp.zeros_like(n_scr)
        m_scr[...] = jnp.zeros_like(m_scr)

    rows = lax.broadcasted_iota(jnp.int32, (L, L), 0)
    cols = lax.broadcasted_iota(jnp.int32, (L, L), 1)
    keep = (cols >= rows) if reverse else (cols <= rows)
    tri = jnp.where(keep, 1.0, 0.0).astype(jnp.bfloat16)

    pre = g_ref[...] + gb_ref[...]
    capped = GATE_CAP * jnp.tanh(pre / GATE_CAP)
    log_f = _log_sigmoid(capped)
    cum = _split3_dot(tri, log_f)
    capped_t = capped.T
    cum_t = cum.T
    edge = 0 if reverse else L - 1
    d = 1 if reverse else 0
    heads = range(M_HEADS)

    q, k, v, c_prev, s, qc = [], [], [], [], [], []
    for h in heads:
        q.append(q_ref[:, h * dk:(h + 1) * dk])
        k.append(k_ref[:, h * dk:(h + 1) * dk] * jnp.asarray(dk ** -0.5, k_ref.dtype))
        v.append(v_ref[:, h * dv:(h + 1) * dv])
        c_prev.append(c_scr[h])
        s.append(lax.dot_general(q[h], k[h], (((1,), (1,)), ((), ())), preferred_element_type=jnp.float32))
        qc.append(jnp.dot(q[h], c_prev[h].astype(jnp.bfloat16), preferred_element_type=jnp.float32))

    p, w_inter, den, m_row = [], [], [], []
    for h in heads:
        il = (2 * d) * M_HEADS + h
        fl = (2 * d + 1) * M_HEADS + h
        b_col = cum[:, fl:fl + 1]
        li_col = capped[:, il:il + 1]
        total = cum[edge:edge + 1, fl:fl + 1]
        src_row = capped_t[il:il + 1, :] - cum_t[fl:fl + 1, :]
        m_prev = m_scr[h, 0:1, 0:1]
        n_prev = n_scr[h, 0:1, :]

        d_log = jnp.where(keep, b_col + src_row, MASK_NEG)
        m_inter = b_col + m_prev
        m_row.append(jnp.maximum(m_inter, jnp.max(d_log, axis=1, keepdims=True)))
        ph = s[h] * jnp.exp(d_log - m_row[h])
        w_inter.append(jnp.exp(m_inter - m_row[h]))
        den.append(w_inter[h] * jnp.sum(q[h].astype(jnp.float32) * n_prev, axis=1, keepdims=True)
                   + jnp.sum(ph, axis=1, keepdims=True))
        p.append(ph.astype(jnp.bfloat16))

        g_col = total - b_col + li_col
        m_new = jnp.maximum(total + m_prev, jnp.max(g_col, axis=0, keepdims=True))
        w_tok = jnp.exp(g_col - m_new)
        decay = jnp.exp(total + m_prev - m_new)
        kw = k[h].astype(jnp.float32) * w_tok
        c_scr[h] = decay * c_prev[h] + lax.dot_general(
            kw.astype(jnp.bfloat16), v[h], (((0,), (0,)), ((), ())), preferred_element_type=jnp.float32)
        n_scr[h] = jnp.broadcast_to(decay * n_prev + jnp.sum(kw, axis=0, keepdims=True), n_scr.shape[1:])
        m_scr[h] = jnp.broadcast_to(m_new, m_scr.shape[1:])

    for h in heads:
        sl = slice(h * dv, (h + 1) * dv)
        num = w_inter[h] * qc[h] + jnp.dot(p[h], v[h], preferred_element_type=jnp.float32)
        x = num / jnp.maximum(jnp.abs(den[h]), jnp.exp(-m_row[h]))
        if combine:
            x = x + hf_ref[:, sl]
            x = x * _rms_scale(x) * gout_ref[:, sl]
            x = x * jax.nn.sigmoid(og_ref[:, sl].astype(jnp.float32))
        o_ref[:, sl] = x.astype(o_ref.dtype)


def mlstm_scan(z, gates, gate_bias, seqs, reverse, dk, dv, combine=None):
    m = z.shape[0]
    L = MLSTM_CHUNK
    qw, vw = M_HEADS * dk, M_HEADS * dv
    n_chunks = m // L
    for s0, t in seqs:
        assert s0 % L == 0 and t % L == 0
    if reverse:
        firsts = tuple((s0 + t) // L - 1 for s0, t in seqs)
    else:
        firsts = tuple(s0 // L for s0, _ in seqs)
    cidx = (lambda i: n_chunks - 1 - i) if reverse else (lambda i: i)
    assert (2 * qw) % vw == 0
    v_blk = (2 * qw) // vw
    in_specs = [pl.BlockSpec((L, qw), lambda i: (cidx(i), 0)),
                pl.BlockSpec((L, qw), lambda i: (cidx(i), 1)),
                pl.BlockSpec((L, vw), lambda i: (cidx(i), v_blk)),
                pl.BlockSpec((L, GATE_LANES), lambda i: (cidx(i), 0)),
                pl.BlockSpec((1, GATE_LANES), lambda i: (0, 0))]
    args = [z, z, z, gates, gate_bias]
    blk = (2 * _nbytes((L, qw), z.dtype) + _nbytes((L, vw), z.dtype) + _nbytes((L, GATE_LANES), jnp.float32)
           + _nbytes((L, vw), jnp.float32) + _nbytes((M_HEADS, dk, dv), jnp.float32))
    out_dtype = jnp.float32
    if combine is not None:
        h_other, g_out = combine
        in_specs += [pl.BlockSpec((L, vw), lambda i: (cidx(i), 0)),
                     pl.BlockSpec((L, vw), lambda i: (cidx(i), v_blk + 1)),
                     pl.BlockSpec((1, vw), lambda i: (0, 0))]
        args += [h_other, z, g_out.reshape(1, vw)]
        blk += _nbytes((L, vw), jnp.float32) + _nbytes((L, vw), z.dtype)
        out_dtype = jnp.bfloat16
    scratch = [pltpu.VMEM((M_HEADS, dk, dv), jnp.float32),
               pltpu.VMEM((M_HEADS, 8, dk), jnp.float32),
               pltpu.VMEM((M_HEADS, 8, LANE), jnp.float32)]
    return pl.pallas_call(
        functools.partial(_mlstm_kernel, reverse=reverse, n_chunks=n_chunks,
                          seq_first_chunks=firsts, dk=dk, dv=dv, combine=combine is not None),
        out_shape=jax.ShapeDtypeStruct((m, vw), out_dtype),
        grid=(n_chunks,),
        in_specs=in_specs,
        out_specs=pl.BlockSpec((L, vw), lambda i: (cidx(i), 0)),
        scratch_shapes=scratch,
        compiler_params=_params(blk, ("arbitrary",)),
        name="mlstm_bwd" if reverse else "mlstm_fwd",
    )(*args)


NA_PAIR = 2
NA_WIN = NA_KR + NA_PAIR - 1
NA_CASES = ((0, 0, 0), (2, 0, 0), (4, 0, 1), (5, 1, 1), (7, 1, 1))
NA_HEAD_GROUP = 4
LOG2E = float(np.log2(np.e))


def _na_kernel(q_ref, k_ref, v_ref, bias_ref, o_ref, *, dh):
    scale2 = dh ** -0.5 * LOG2E
    for h0 in range(0, NA_HEADS, NA_HEAD_GROUP):
        heads = range(h0, h0 + NA_HEAD_GROUP)
        s = {}
        for h in heads:
            sl = slice(h * dh, (h + 1) * dh)
            s[h] = lax.dot_general(q_ref[:, sl], k_ref[:, sl], (((1,), (1,)), ((), ())),
                                   preferred_element_type=jnp.float32)
        e, inv = {}, {}
        for h in heads:
            t = s[h] * scale2 + bias_ref[0, h]
            eh = jnp.exp2(t - jnp.max(t, axis=1, keepdims=True))
            inv[h] = 1.0 / jnp.sum(eh, axis=1, keepdims=True)
            e[h] = eh.astype(jnp.bfloat16)
        for h in heads:
            sl = slice(h * dh, (h + 1) * dh)
            o = jnp.dot(e[h], v_ref[:, sl], preferred_element_type=jnp.float32) * inv[h]
            o_ref[:, sl] = o.astype(o_ref.dtype)


def _na_bias_tables(rpb):
    depth = rpb.shape[0]
    qs = np.arange(GRID_W)
    cs = np.clip(qs - NA_KC // 2, 0, GRID_W - NA_KC)
    ccol = np.arange(GRID_W)
    col_ok = (ccol[None, :] >= cs[:, None]) & (ccol[None, :] < cs[:, None] + NA_KC)
    dc = ccol[None, :] - qs[:, None] + NA_KC - 1
    n_dr, n_dc = 2 * NA_KR - 1, 2 * NA_KC - 1
    col_sel = (dc[None] == np.arange(n_dc)[:, None, None]) & col_ok[None]
    planes = jnp.dot(rpb.astype(jnp.float32).reshape(-1, n_dc),
                     col_sel.reshape(n_dc, GRID_W * GRID_W).astype(jnp.float32),
                     precision=lax.Precision.HIGHEST)
    planes = planes.reshape(depth, NA_HEADS, n_dr, GRID_W, GRID_W)
    planes = jnp.where(col_ok, planes, MASK_NEG)
    masked = jnp.full((depth, NA_HEADS, GRID_W, GRID_W), MASK_NEG, jnp.float32)
    cases = []
    for rel0, *offs in NA_CASES:
        members = []
        for j in range(NA_PAIR):
            rows = [planes[:, :, i - rel0 - j + NA_KR - 1] if offs[j] <= i < offs[j] + NA_KR else masked
                    for i in range(NA_WIN)]
            members.append(jnp.stack(rows, axis=3))
        cases.append(jnp.stack(members, axis=2))
    t = jnp.stack(cases, axis=1) * LOG2E
    return t.reshape(depth, len(NA_CASES), NA_HEADS, NA_PAIR * GRID_W, NA_WIN * GRID_W)


def neighborhood_attention(z, bias, seqs, q_col_block, dh):
    m = z.shape[0]
    hw = NA_HEADS * dh
    n_rows = m // GRID_W
    bounds = [(s0 // GRID_W, t // GRID_W) for s0, t in seqs]
    for r0, nr in bounds:
        assert nr % NA_PAIR == 0 and r0 % NA_PAIR == 0 and nr >= NA_KR + 4

    def geometry(g):
        r = g * NA_PAIR
        lo = jnp.int32(0)
        nr_ = jnp.int32(0)
        for r0, nr in bounds:
            inside = (r >= r0) & (r < r0 + nr)
            lo = jnp.where(inside, r0, lo)
            nr_ = jnp.where(inside, nr, nr_)
        top = r - lo
        bot = lo + nr_ - NA_PAIR - r
        u = jnp.clip(r - NA_KR // 2, lo, lo + nr_ - NA_WIN)
        case = jnp.where(top == 0, 0, jnp.where(top == 2, 1, jnp.where(bot == 0, 4, jnp.where(bot == 2, 3, 2))))
        return u, case

    tq, tk = NA_PAIR * GRID_W, NA_WIN * GRID_W

    def kv_spec(col_block):
        return pl.BlockSpec((pl.Element(tk), pl.Element(hw)),
                            lambda g: (geometry(g)[0] * GRID_W, col_block * hw))

    in_specs = [pl.BlockSpec((tq, hw), lambda g: (g, q_col_block)),
                kv_spec(q_col_block + 1), kv_spec(q_col_block + 2),
                pl.BlockSpec((1, NA_HEADS, tq, tk), lambda g: (geometry(g)[1], 0, 0, 0))]
    blk = (2 * _nbytes((tq, hw), z.dtype) + 2 * _nbytes((tk, hw), z.dtype)
           + 2 * _nbytes((NA_HEADS, tq, tk), jnp.float32))
    return pl.pallas_call(
        functools.partial(_na_kernel, dh=dh),
        out_shape=jax.ShapeDtypeStruct((m, hw), jnp.bfloat16),
        grid=(n_rows // NA_PAIR,),
        in_specs=in_specs,
        out_specs=pl.BlockSpec((tq, hw), lambda g: (g, 0)),
        compiler_params=_params(blk, ("parallel",)),
        name="neighborhood_attention",
    )(z, z, z, bias)


def _xattn_kernel(q_ref, k_ref, v_ref, o_ref, *, dh):
    scale = dh ** -0.5
    for h in range(X_HEADS):
        sl = slice(h * dh, (h + 1) * dh)
        s = lax.dot_general(q_ref[:, sl], k_ref[:, sl], (((1,), (1,)), ((), ())),
                            preferred_element_type=jnp.float32) * scale
        s = s - jnp.max(s, axis=1, keepdims=True)
        e = jnp.exp(s)
        p = e / jnp.sum(e, axis=1, keepdims=True)
        o_ref[:, sl] = jnp.dot(p.astype(v_ref.dtype), v_ref[:, sl],
                               preferred_element_type=jnp.float32).astype(o_ref.dtype)


def memory_attention(q, k, v, seqs, n_mem):
    m, xw = q.shape
    dh = xw // X_HEADS
    tm = XATTN_TILE
    for s0, t in seqs:
        tm = min(tm, _tile(t, tm))
    starts = [s0 // tm for s0, _ in seqs]
    for s0, _ in seqs:
        assert s0 % tm == 0

    def seq_of(i):
        s = jnp.int32(0)
        for idx, st in enumerate(starts):
            s = jnp.where(i >= st, idx, s)
        return s

    blk = 2 * _nbytes((tm, xw), q.dtype) + 2 * _nbytes((n_mem, xw), k.dtype) + 2 * _nbytes((tm, n_mem), jnp.float32)
    return pl.pallas_call(
        functools.partial(_xattn_kernel, dh=dh),
        out_shape=jax.ShapeDtypeStruct((m, xw), jnp.bfloat16),
        grid=(m // tm,),
        in_specs=[pl.BlockSpec((tm, xw), lambda i: (i, 0)),
                  pl.BlockSpec((n_mem, xw), lambda i: (seq_of(i), 0)),
                  pl.BlockSpec((n_mem, xw), lambda i: (seq_of(i), 0))],
        out_specs=pl.BlockSpec((tm, xw), lambda i: (i, 0)),
        compiler_params=_params(blk, ("parallel",)),
        name="memory_attention",
    )(q, k, v)


def kernel(x_prompt, x_sample, mem_prompt, mem_sample, w_in, w_out, g_pre_mix, g_post_mix, i_bias, f_bias, g_mlstm_out, rpb, g_pre_xattn, g_post_xattn, g_mem, wq_x, wk_x, wv_x, wo_x, g_pre_ffn, g_post_ffn, w_gate, w_up, w_down):
    bf16 = jnp.bfloat16
    depth = w_in.shape[0]
    d = x_prompt.shape[-1]
    bp, tp, _ = x_prompt.shape
    bs, ts, _ = x_sample.shape
    n_mem = mem_prompt.shape[1]
    mix_w = w_out.shape[1]
    v_w = mix_w // 2
    dv = v_w // M_HEADS
    dk = dv // 2
    qk_w = M_HEADS * dk
    na_dh = v_w // NA_HEADS
    n_gates = 4 * M_HEADS
    d_ff = w_gate.shape[2]
    assert w_in.shape[2] == 2 * qk_w + 2 * v_w + n_gates + 3 * v_w

    seqs = [(b * tp, tp) for b in range(bp)] + [(bp * tp + b * ts, ts) for b in range(bs)]
    h = jnp.concatenate([x_prompt.reshape(bp * tp, d), x_sample.reshape(bs * ts, d)], axis=0)
    mem = jnp.concatenate([mem_prompt.reshape(bp * n_mem, d), mem_sample.reshape(bs * n_mem, d)], axis=0)

    m_cols = 2 * qk_w + 2 * v_w
    w_main = jnp.concatenate([w_in[:, :, :m_cols], w_in[:, :, m_cols + n_gates:]], axis=2).astype(bf16)
    w_gates = jnp.pad(w_in[:, :, m_cols:m_cols + n_gates], ((0, 0), (0, 0), (0, GATE_LANES - n_gates))).astype(bf16)
    gate_bias = jnp.concatenate([i_bias[:, 0], f_bias[:, 0], i_bias[:, 1], f_bias[:, 1]], axis=1)
    gate_bias = jnp.pad(gate_bias, ((0, 0), (0, GATE_LANES - n_gates))).astype(jnp.float32)
    w_out_m = w_out[:, :v_w].astype(bf16)
    w_out_n = w_out[:, v_w:].astype(bf16)
    wq, wk, wv, wo = (w.astype(bf16) for w in (wq_x, wk_x, wv_x, wo_x))
    wg, wu, wd = (w.astype(bf16) for w in (w_gate, w_up, w_down))
    na_bias = _na_bias_tables(rpb)
    ff_splits = 2 if (d_ff // 2) % LANE == 0 and d_ff > 4096 else 1

    a = rmsnorm_cast(h, g_pre_mix[0])
    for l in range(depth):
        z = matmul([a], [w_main[l]], bf16)
        gates = matmul([a], [w_gates[l]], jnp.float32, tn_pref=GATE_LANES)
        gb = gate_bias[l].reshape(1, GATE_LANES)
        h_fw = mlstm_scan(z, gates, gb, seqs, False, dk, dv)
        y_m = mlstm_scan(z, gates, gb, seqs, True, dk, dv, combine=(h_fw, g_mlstm_out[l]))
        y_n = neighborhood_attention(z, na_bias[l], seqs, m_cols // v_w, na_dh)
        y = matmul([y_m, y_n], [w_out_m[l], w_out_n[l]], jnp.float32)
        h, a = residual_norm(h, y, g_post_mix[l], g_pre_xattn[l])
        mem_n = rmsnorm_cast(mem, g_mem[l])
        q = matmul([a], [wq[l]], bf16)
        k = matmul([mem_n], [wk[l]], bf16)
        v = matmul([mem_n], [wv[l]], bf16)
        o = memory_attention(q, k, v, seqs, n_mem)
        y = matmul([o], [wo[l]], jnp.float32)
        h, a = residual_norm(h, y, g_post_xattn[l], g_pre_ffn[l])
        hidden = ffn_gate_up(a, wg[l], wu[l])
        y = matmul_ksplit(hidden, wd[l], ff_splits)
        h, a = residual_norm(h, y, g_post_ffn[l], g_pre_mix[l + 1] if l + 1 < depth else None)

    return (h[:bp * tp].reshape(bp, tp, d), h[bp * tp:].reshape(bs, ts, d))
```

```python
import functools

import jax
import jax.numpy as jnp
import numpy as np
from jax import lax
from jax.experimental import pallas as pl
from jax.experimental.pallas import tpu as pltpu

GRID_W = 64
M_HEADS = 4
GATE_CAP = 15.0
NA_HEADS = 16
NA_KR = 8
NA_KC = 16
X_HEADS = 4
RMS_EPS = 1e-6
GATE_LANES = 128
MASK_NEG = -1e30

V7X_VMEM_BYTES = 64 * 1024 * 1024
V7X_VMEM_BUDGET = 56 * 1024 * 1024
LANE = 128

MLSTM_CHUNK = 256
TOK_TILE = 1024
ELEM_TILE = 256
XATTN_TILE = 512
QPROJ_TILE = 256
IN_PROJ_TN = 512
IN_PROJ_CHUNKS = 16
FF_UP_TN = 256
FF_UP_CHUNKS = 32
FF_DOWN_TILE = 512


def _params(block_bytes, semantics):
    limit = min(V7X_VMEM_BUDGET, int(2 * block_bytes) + 16 * 1024 * 1024)
    return pltpu.CompilerParams(dimension_semantics=semantics, vmem_limit_bytes=limit)


def _nbytes(shape, dtype):
    return int(np.prod(shape)) * jnp.dtype(dtype).itemsize


def _tile(m, pref):
    t = min(m, pref)
    while m % t:
        t //= 2
    return t


def _rms_scale(x):
    return lax.rsqrt(jnp.mean(x * x, axis=-1, keepdims=True) + RMS_EPS)


def _rmsnorm_cast_kernel(x_ref, g_ref, o_ref):
    x = x_ref[...]
    o_ref[...] = (x * _rms_scale(x) * g_ref[...]).astype(o_ref.dtype)


def rmsnorm_cast(x, g):
    m, d = x.shape
    tm = _tile(m, ELEM_TILE)
    blk = _nbytes((tm, d), jnp.float32) + _nbytes((tm, d), jnp.bfloat16)
    return pl.pallas_call(
        _rmsnorm_cast_kernel,
        out_shape=jax.ShapeDtypeStruct((m, d), jnp.bfloat16),
        grid=(m // tm,),
        in_specs=[pl.BlockSpec((tm, d), lambda i: (i, 0)),
                  pl.BlockSpec((1, d), lambda i: (0, 0))],
        out_specs=pl.BlockSpec((tm, d), lambda i: (i, 0)),
        compiler_params=_params(blk, ("parallel",)),
        name="rmsnorm_cast",
    )(x, g.reshape(1, d))


def _residual_split_kernel(h_ref, y_ref, gpost_ref, a_out_ref, b_out_ref, *, n_first):
    y = y_ref[...]
    x = h_ref[...] + y * _rms_scale(y) * gpost_ref[...]
    i = pl.program_id(0)

    @pl.when(i < n_first)
    def _():
        a_out_ref[...] = x

    @pl.when(i >= n_first)
    def _():
        b_out_ref[...] = x


def residual_split(h, y, g_post, rows_first):
    m, d = h.shape
    tm = _tile(np.gcd(rows_first, m - rows_first), ELEM_TILE)
    n_first = rows_first // tm
    row = pl.BlockSpec((tm, d), lambda i: (i, 0))
    vec = pl.BlockSpec((1, d), lambda i: (0, 0))
    blk = 4 * _nbytes((tm, d), jnp.float32)
    return pl.pallas_call(
        functools.partial(_residual_split_kernel, n_first=n_first),
        out_shape=(jax.ShapeDtypeStruct((rows_first, d), jnp.float32),
                   jax.ShapeDtypeStruct((m - rows_first, d), jnp.float32)),
        grid=(m // tm,), in_specs=[row, row, vec],
        out_specs=(pl.BlockSpec((tm, d), lambda i: (jnp.minimum(i, n_first - 1), 0)),
                   pl.BlockSpec((tm, d), lambda i: (jnp.maximum(i - n_first, 0), 0))),
        compiler_params=_params(blk, ("arbitrary",)),
        name="residual_split",
    )(h, y, g_post.reshape(1, d))


def _norm_ahead_kernel(*refs, n_src, src_split, has_y, mode, n_row_tiles, n_chunks, chunk_rows):
    refs = list(refs)
    src_refs = [refs.pop(0) for _ in range(n_src)]
    y_ref, gpost_ref = (refs.pop(0), refs.pop(0)) if has_y else (None, None)
    gpre_ref = refs.pop(0)
    if mode == "in_proj":
        w_ref, wx_ref, h_out_ref, o_ref, ox_ref, a_even, a_odd = refs
    elif mode == "gate_up":
        w_ref, wu_ref, h_out_ref, o_ref, a_even, a_odd = refs
    else:
        w_ref, h_out_ref, o_ref, a_even, a_odd = refs
    r = pl.program_id(0)
    j = pl.program_id(1)

    def norm_chunk(a_dst):
        cj = jnp.minimum(j, n_chunks - 1)
        if n_src == 2:
            x = jnp.where(r * n_chunks + cj < src_split, src_refs[0][...], src_refs[1][...])
        else:
            x = src_refs[0][...]
        if has_y:
            y = y_ref[...]
            x = x + y * _rms_scale(y) * gpost_ref[...]
        h_out_ref[...] = x
        a = (x * _rms_scale(x) * gpre_ref[...]).astype(a_dst.dtype)
        a_dst[pl.ds(pl.multiple_of(cj * chunk_rows, chunk_rows), chunk_rows), :] = a

    def project(a_src):
        a = a_src[...]
        if mode == "gate_up":
            g = jnp.dot(a, w_ref[...], preferred_element_type=jnp.float32)
            u = jnp.dot(a, wu_ref[...], preferred_element_type=jnp.float32)
            o_ref[...] = (g * jax.nn.sigmoid(g) * u).astype(o_ref.dtype)
        else:
            o_ref[...] = jnp.dot(a, w_ref[...], preferred_element_type=jnp.float32).astype(o_ref.dtype)

    @pl.when(r == 0)
    def _():
        norm_chunk(a_even)

    for parity, (a_dst, a_src) in enumerate(((a_even, a_odd), (a_odd, a_even))):
        @pl.when((r > 0) & (r % 2 == parity))
        def _(a_dst=a_dst, a_src=a_src):
            norm_chunk(a_dst)
            project(a_src)

        if mode == "in_proj":
            @pl.when((r > 0) & (r % 2 == parity) & (j == 0))
            def _(a_src=a_src):
                ox_ref[...] = jnp.dot(a_src[...], wx_ref[...], preferred_element_type=jnp.float32)


def norm_ahead_proj(srcs, y, g_post, g_pre, weights, mode, tm, tn, n_chunks):
    d = srcs[0].shape[1]
    m = sum(s.shape[0] for s in srcs)
    n = weights[0].shape[1]
    assert m % tm == 0 and n % tn == 0 and tm % n_chunks == 0
    n_row_tiles, n_col_tiles = m // tm, n // tn
    assert n_chunks <= n_col_tiles
    chunk_rows = tm // n_chunks
    total_chunks = n_row_tiles * n_chunks
    src_split = srcs[0].shape[0] // chunk_rows
    for s in srcs:
        assert s.shape[0] % chunk_rows == 0

    def chunk(r, j):
        return jnp.minimum(r * n_chunks + jnp.minimum(j, n_chunks - 1), total_chunks - 1)

    def col(r, j):
        return jnp.where(r == 0, 0, j)

    row_blk = (chunk_rows, d)
    if len(srcs) == 2:
        in_specs = [pl.BlockSpec(row_blk, lambda r, j: (jnp.minimum(chunk(r, j), src_split - 1), 0)),
                    pl.BlockSpec(row_blk, lambda r, j: (jnp.maximum(chunk(r, j) - src_split, 0), 0))]
    else:
        in_specs = [pl.BlockSpec(row_blk, lambda r, j: (chunk(r, j), 0))]
    args = list(srcs)
    vec = pl.BlockSpec((1, d), lambda r, j: (0, 0))
    if y is not None:
        in_specs += [pl.BlockSpec(row_blk, lambda r, j: (chunk(r, j), 0)), vec]
        args += [y, g_post.reshape(1, d)]
    in_specs.append(vec)
    args.append(g_pre.reshape(1, d))
    w_spec = pl.BlockSpec((d, tn), lambda r, j: (0, col(r, j)))
    out_spec = pl.BlockSpec((tm, tn), lambda r, j: (jnp.maximum(r - 1, 0), col(r, j)))
    out_shape = [jax.ShapeDtypeStruct((m, d), jnp.float32), jax.ShapeDtypeStruct((m, n), jnp.bfloat16)]
    out_specs = [pl.BlockSpec(row_blk, lambda r, j: (chunk(r, j), 0)), out_spec]
    blk = ((len(srcs) + (y is not None) + 1) * _nbytes(row_blk, jnp.float32) + _nbytes((d, tn), jnp.bfloat16)
           + _nbytes((tm, tn), jnp.bfloat16) + 2 * _nbytes((tm, tn), jnp.float32) + _nbytes((tm, d), jnp.bfloat16))
    if mode == "in_proj":
        nx = weights[1].shape[1]
        in_specs += [w_spec, pl.BlockSpec((d, nx), lambda r, j: (0, 0))]
        out_shape.append(jax.ShapeDtypeStruct((m, nx), jnp.float32))
        out_specs.append(pl.BlockSpec((tm, nx), lambda r, j: (jnp.maximum(r - 1, 0), 0)))
        blk += _nbytes((d, nx), jnp.bfloat16) + _nbytes((tm, nx), jnp.float32)
    elif mode == "gate_up":
        in_specs += [w_spec, w_spec]
        blk += _nbytes((d, tn), jnp.bfloat16) + 2 * _nbytes((tm, tn), jnp.float32)
    else:
        in_specs.append(w_spec)
    args += list(weights)
    return pl.pallas_call(
        functools.partial(_norm_ahead_kernel, n_src=len(srcs), src_split=src_split, has_y=y is not None,
                          mode=mode, n_row_tiles=n_row_tiles, n_chunks=n_chunks, chunk_rows=chunk_rows),
        out_shape=tuple(out_shape),
        grid=(n_row_tiles + 1, n_col_tiles),
        in_specs=in_specs,
        out_specs=tuple(out_specs),
        scratch_shapes=[pltpu.VMEM((tm, d), jnp.bfloat16), pltpu.VMEM((tm, d), jnp.bfloat16)],
        compiler_params=_params(blk, ("arbitrary", "arbitrary")),
        name="norm_ahead_" + mode,
    )(*args)


def _matmul_kernel(*refs, n_pairs):
    o_ref = refs[-1]
    acc = jnp.dot(refs[0][...], refs[n_pairs][...], preferred_element_type=jnp.float32)
    for p in range(1, n_pairs):
        acc += jnp.dot(refs[p][...], refs[n_pairs + p][...], preferred_element_type=jnp.float32)
    o_ref[...] = acc.astype(o_ref.dtype)


def matmul(a_list, w_list, out_dtype, tn_pref=1024, tm_pref=TOK_TILE):
    m = a_list[0].shape[0]
    n = w_list[0].shape[1]
    tm = _tile(m, tm_pref)
    tn = _tile(n, tn_pref)
    in_specs, blk = [], 0
    for a in a_list:
        k = a.shape[1]
        in_specs.append(pl.BlockSpec((tm, k), lambda i, j: (i, 0)))
        blk += _nbytes((tm, k), a.dtype)
    for w in w_list:
        k = w.shape[0]
        in_specs.append(pl.BlockSpec((k, tn), lambda i, j: (0, j)))
        blk += _nbytes((k, tn), w.dtype)
    blk += _nbytes((tm, tn), out_dtype) + _nbytes((tm, tn), jnp.float32)
    return pl.pallas_call(
        functools.partial(_matmul_kernel, n_pairs=len(a_list)),
        out_shape=jax.ShapeDtypeStruct((m, n), out_dtype),
        grid=(m // tm, n // tn),
        in_specs=in_specs,
        out_specs=pl.BlockSpec((tm, tn), lambda i, j: (i, j)),
        compiler_params=_params(blk, ("parallel", "arbitrary")),
        name="matmul",
    )(*a_list, *w_list)


def _log_sigmoid(x):
    return jnp.minimum(x, 0.0) - jnp.log(1.0 + jnp.exp(-jnp.abs(x)))


def _split3_dot(tri, x):
    hi = x.astype(jnp.bfloat16)
    r1 = x - hi.astype(jnp.float32)
    mid = r1.astype(jnp.bfloat16)
    lo = (r1 - mid.astype(jnp.float32)).astype(jnp.bfloat16)
    out = jnp.dot(tri, hi, preferred_element_type=jnp.float32)
    out += jnp.dot(tri, mid, preferred_element_type=jnp.float32)
    out += jnp.dot(tri, lo, preferred_element_type=jnp.float32)
    return out


def _mlstm_kernel(*refs, reverse, n_chunks, seq_first_chunks, dk, dv, combine):
    if combine:
        q_ref, k_ref, v_ref, g_ref, gb_ref, hf_ref, og_ref, gout_ref, o_ref, c_scr, n_scr, m_scr = refs
    else:
        q_ref, k_ref, v_ref, g_ref, gb_ref, o_ref, c_scr, n_scr, m_scr = refs
    L = q_ref.shape[0]
    step = pl.program_id(0)
    chunk = (n_chunks - 1 - step) if reverse else step

    is_start = functools.reduce(jnp.logical_or, [chunk == c for c in seq_first_chunks])

    @pl.when(is_start)
    def _():
        c_scr[...] = jnp.zeros_like(c_scr)
        n_scr[...] = jnp.zeros_like(n_scr)
        m_scr[...] = jnp.zeros_like(m_scr)

    rows = lax.broadcasted_iota(jnp.int32, (L, L), 0)
    cols = lax.broadcasted_iota(jnp.int32, (L, L), 1)
    keep = (cols >= rows) if reverse else (cols <= rows)
    tri = jnp.where(keep, 1.0, 0.0).astype(jnp.bfloat16)

    pre = g_ref[...] + gb_ref[...]
    capped = GATE_CAP * jnp.tanh(pre / GATE_CAP)
    log_f = _log_sigmoid(capped)
    cum = _split3_dot(tri, log_f)
    capped_t = capped.T
    cum_t = cum.T
    edge = 0 if reverse else L - 1
    d = 1 if reverse else 0
    heads = range(M_HEADS)

    q, k, v, c_prev, s, qc = [], [], [], [], [], []
    for h in heads:
        q.append(q_ref[:, h * dk:(h + 1) * dk])
        k.append(k_ref[:, h * dk:(h + 1) * dk] * jnp.asarray(dk ** -0.5, k_ref.dtype))
        v.append(v_ref[:, h * dv:(h + 1) * dv])
        c_prev.append(c_scr[h])
        s.append(lax.dot_general(q[h], k[h], (((1,), (1,)), ((), ())), preferred_element_type=jnp.float32))
        qc.append(jnp.dot(q[h], c_prev[h].astype(jnp.bfloat16), preferred_element_type=jnp.float32))

    p, w_inter, den, m_row = [], [], [], []
    for h in heads:
        il = (2 * d) * M_HEADS + h
        fl = (2 * d + 1) * M_HEADS + h
        b_col = cum[:, fl:fl + 1]
        li_col = capped[:, il:il + 1]
        total = cum[edge:edge + 1, fl:fl + 1]
        src_row = capped_t[il:il + 1, :] - cum_t[fl:fl + 1, :]
        m_prev = m_scr[h, 0:1, 0:1]
        n_prev = n_scr[h, 0:1, :]

        d_log = jnp.where(keep, b_col + src_row, MASK_NEG)
        m_inter = b_col + m_prev
        m_row.append(jnp.maximum(m_inter, jnp.max(d_log, axis=1, keepdims=True)))
        ph = s[h] * jnp.exp(d_log - m_row[h])
        w_inter.append(jnp.exp(m_inter - m_row[h]))
        den.append(w_inter[h] * jnp.sum(q[h].astype(jnp.float32) * n_prev, axis=1, keepdims=True)
                   + jnp.sum(ph, axis=1, keepdims=True))
        p.append(ph.astype(jnp.bfloat16))

        g_col = total - b_col + li_col
        m_new = jnp.maximum(total + m_prev, jnp.max(g_col, axis=0, keepdims=True))
        w_tok = jnp.exp(g_col - m_new)
        decay = jnp.exp(total + m_prev - m_new)
        kw = k[h].astype(jnp.float32) * w_tok
        c_scr[h] = decay * c_prev[h] + lax.dot_general(
            kw.astype(jnp.bfloat16), v[h], (((0,), (0,)), ((), ())), preferred_element_type=jnp.float32)
        n_scr[h] = jnp.broadcast_to(decay * n_prev + jnp.sum(kw, axis=0, keepdims=True), n_scr.shape[1:])
        m_scr[h] = jnp.broadcast_to(m_new, m_scr.shape[1:])

    for h in heads:
        sl = slice(h * dv, (h + 1) * dv)
        num = w_inter[h] * qc[h] + jnp.dot(p[h], v[h], preferred_element_type=jnp.float32)
        x = num / jnp.maximum(jnp.abs(den[h]), jnp.exp(-m_row[h]))
        if combine:
            x = x + hf_ref[:, sl]
            x = x * _rms_scale(x) * gout_ref[:, sl]
            x = x * jax.nn.sigmoid(og_ref[:, sl].astype(jnp.float32))
        o_ref[:, sl] = x.astype(o_ref.dtype)


def mlstm_scan(z, gates, gate_bias, seqs, reverse, dk, dv, combine=None):
    m = z.shape[0]
    L = MLSTM_CHUNK
    qw, vw = M_HEADS * dk, M_HEADS * dv
    n_chunks = m // L
    for s0, t in seqs:
        assert s0 % L == 0 and t % L == 0
    if reverse:
        firsts = tuple((s0 + t) // L - 1 for s0, t in seqs)
    else:
        firsts = tuple(s0 // L for s0, _ in seqs)
    cidx = (lambda i: n_chunks - 1 - i) if reverse else (lambda i: i)
    assert (2 * qw) % vw == 0
    v_blk = (2 * qw) // vw
    in_specs = [pl.BlockSpec((L, qw), lambda i: (cidx(i), 0)),
                pl.BlockSpec((L, qw), lambda i: (cidx(i), 1)),
                pl.BlockSpec((L, vw), lambda i: (cidx(i), v_blk)),
                pl.BlockSpec((L, GATE_LANES), lambda i: (cidx(i), 0)),
                pl.BlockSpec((1, GATE_LANES), lambda i: (0, 0))]
    args = [z, z, z, gates, gate_bias]
    blk = (2 * _nbytes((L, qw), z.dtype) + _nbytes((L, vw), z.dtype) + _nbytes((L, GATE_LANES), jnp.float32)
           + _nbytes((L, vw), jnp.float32) + _nbytes((M_HEADS, dk, dv), jnp.float32))
    out_dtype = jnp.float32
    if combine is not None:
        h_other, g_out = combine
        in_specs += [pl.BlockSpec((L, vw), lambda i: (cidx(i), 0)),
                     pl.BlockSpec((L, vw), lambda i: (cidx(i), v_blk + 1)),
                     pl.BlockSpec((1, vw), lambda i: (0, 0))]
        args += [h_other, z, g_out.reshape(1, vw)]
        blk += _nbytes((L, vw), jnp.float32) + _nbytes((L, vw), z.dtype)
        out_dtype = jnp.bfloat16
    scratch = [pltpu.VMEM((M_HEADS, dk, dv), jnp.float32),
               pltpu.VMEM((M_HEADS, 8, dk), jnp.float32),
               pltpu.VMEM((M_HEADS, 8, LANE), jnp.float32)]
    return pl.pallas_call(
        functools.partial(_mlstm_kernel, reverse=reverse, n_chunks=n_chunks,
                          seq_first_chunks=firsts, dk=dk, dv=dv, combine=combine is not None),
        out_shape=jax.ShapeDtypeStruct((m, vw), out_dtype),
        grid=(n_chunks,),
        in_specs=in_specs,
        out_specs=pl.BlockSpec((L, vw), lambda i: (cidx(i), 0)),
        scratch_shapes=scratch,
        compiler_params=_params(blk, ("arbitrary",)),
        name="mlstm_bwd" if reverse else "mlstm_fwd",
    )(*args)


NA_PAIR = 2
NA_WIN = NA_KR + NA_PAIR - 1
NA_CASES = ((0, 0, 0), (2, 0, 0), (4, 0, 1), (5, 1, 1), (7, 1, 1))
NA_HEAD_GROUP = 4
LOG2E = float(np.log2(np.e))


def _na_kernel(q_ref, k_ref, v_ref, bias_ref, o_ref, *, dh):
    scale2 = dh ** -0.5 * LOG2E
    for h0 in range(0, NA_HEADS, NA_HEAD_GROUP):
        heads = range(h0, h0 + NA_HEAD_GROUP)
        s = {}
        for h in heads:
            sl = slice(h * dh, (h + 1) * dh)
            s[h] = lax.dot_general(q_ref[:, sl], k_ref[:, sl], (((1,), (1,)), ((), ())),
                                   preferred_element_type=jnp.float32)
        e, inv = {}, {}
        for h in heads:
            t = s[h] * scale2 + bias_ref[0, h]
            eh = jnp.exp2(t - jnp.max(t, axis=1, keepdims=True))
            inv[h] = 1.0 / jnp.sum(eh, axis=1, keepdims=True)
            e[h] = eh.astype(jnp.bfloat16)
        for h in heads:
            sl = slice(h * dh, (h + 1) * dh)
            o = jnp.dot(e[h], v_ref[:, sl], preferred_element_type=jnp.float32) * inv[h]
            o_ref[:, sl] = o.astype(o_ref.dtype)


def _na_bias_tables(rpb):
    depth = rpb.shape[0]
    qs = np.arange(GRID_W)
    cs = np.clip(qs - NA_KC // 2, 0, GRID_W - NA_KC)
    ccol = np.arange(GRID_W)
    col_ok = (ccol[None, :] >= cs[:, None]) & (ccol[None, :] < cs[:, None] + NA_KC)
    dc = ccol[None, :] - qs[:, None] + NA_KC - 1
    n_dr, n_dc = 2 * NA_KR - 1, 2 * NA_KC - 1
    col_sel = (dc[None] == np.arange(n_dc)[:, None, None]) & col_ok[None]
    planes = jnp.dot(rpb.astype(jnp.float32).reshape(-1, n_dc),
                     col_sel.reshape(n_dc, GRID_W * GRID_W).astype(jnp.float32),
                     precision=lax.Precision.HIGHEST)
    planes = planes.reshape(depth, NA_HEADS, n_dr, GRID_W, GRID_W)
    planes = jnp.where(col_ok, planes, MASK_NEG) * LOG2E
    planes = jnp.transpose(planes, (0, 1, 3, 2, 4))
    cases = []
    for rel0, *offs in NA_CASES:
        members = []
        for j in range(NA_PAIR):
            dr0 = offs[j] - rel0 - j + NA_KR - 1
            rows = jnp.pad(planes[:, :, :, dr0:dr0 + NA_KR], ((0, 0),) * 3 + ((offs[j], NA_WIN - NA_KR - offs[j]), (0, 0)),
                           constant_values=MASK_NEG * LOG2E)
            members.append(rows.reshape(depth, NA_HEADS, GRID_W, NA_WIN * GRID_W))
        cases.append(jnp.stack(members, axis=2))
    t = jnp.stack(cases, axis=1)
    return t.reshape(depth, len(NA_CASES), NA_HEADS, NA_PAIR * GRID_W, NA_WIN * GRID_W)


def neighborhood_attention(z, bias, seqs, q_col_block, dh):
    m = z.shape[0]
    hw = NA_HEADS * dh
    n_rows = m // GRID_W
    bounds = [(s0 // GRID_W, t // GRID_W) for s0, t in seqs]
    for r0, nr in bounds:
        assert nr % NA_PAIR == 0 and r0 % NA_PAIR == 0 and nr >= NA_KR + 4

    def geometry(g):
        r = g * NA_PAIR
        lo = jnp.int32(0)
        nr_ = jnp.int32(0)
        for r0, nr in bounds:
            inside = (r >= r0) & (r < r0 + nr)
            lo = jnp.where(inside, r0, lo)
            nr_ = jnp.where(inside, nr, nr_)
        top = r - lo
        bot = lo + nr_ - NA_PAIR - r
        u = jnp.clip(r - NA_KR // 2, lo, lo + nr_ - NA_WIN)
        case = jnp.where(top == 0, 0, jnp.where(top == 2, 1, jnp.where(bot == 0, 4, jnp.where(bot == 2, 3, 2))))
        return u, case

    tq, tk = NA_PAIR * GRID_W, NA_WIN * GRID_W

    def kv_spec(col_block):
        return pl.BlockSpec((pl.Element(tk), pl.Element(hw)),
                            lambda g: (geometry(g)[0] * GRID_W, col_block * hw))

    in_specs = [pl.BlockSpec((tq, hw), lambda g: (g, q_col_block)),
                kv_spec(q_col_block + 1), kv_spec(q_col_block + 2),
                pl.BlockSpec((1, NA_HEADS, tq, tk), lambda g: (geometry(g)[1], 0, 0, 0))]
    blk = (2 * _nbytes((tq, hw), z.dtype) + 2 * _nbytes((tk, hw), z.dtype)
           + 2 * _nbytes((NA_HEADS, tq, tk), jnp.float32))
    return pl.pallas_call(
        functools.partial(_na_kernel, dh=dh),
        out_shape=jax.ShapeDtypeStruct((m, hw), jnp.bfloat16),
        grid=(n_rows // NA_PAIR,),
        in_specs=in_specs,
        out_specs=pl.BlockSpec((tq, hw), lambda g: (g, 0)),
        compiler_params=_params(blk, ("parallel",)),
        name="neighborhood_attention",
    )(z, z, z, bias)


def _xattn_kernel(q_ref, k_ref, v_ref, o_ref, *, dh):
    scale = dh ** -0.5
    for h in range(X_HEADS):
        sl = slice(h * dh, (h + 1) * dh)
        s = lax.dot_general(q_ref[:, sl], k_ref[:, sl], (((1,), (1,)), ((), ())),
                            preferred_element_type=jnp.float32) * scale
        s = s - jnp.max(s, axis=1, keepdims=True)
        e = jnp.exp(s)
        p = e / jnp.sum(e, axis=1, keepdims=True)
        o_ref[:, sl] = jnp.dot(p.astype(v_ref.dtype), v_ref[:, sl],
                               preferred_element_type=jnp.float32).astype(o_ref.dtype)


def memory_attention(q, k, v, seqs, n_mem):
    m, xw = q.shape
    dh = xw // X_HEADS
    tm = XATTN_TILE
    for s0, t in seqs:
        tm = min(tm, _tile(t, tm))
    starts = [s0 // tm for s0, _ in seqs]
    for s0, _ in seqs:
        assert s0 % tm == 0

    def seq_of(i):
        s = jnp.int32(0)
        for idx, st in enumerate(starts):
            s = jnp.where(i >= st, idx, s)
        return s

    blk = 2 * _nbytes((tm, xw), q.dtype) + 2 * _nbytes((n_mem, xw), k.dtype) + 2 * _nbytes((tm, n_mem), jnp.float32)
    return pl.pallas_call(
        functools.partial(_xattn_kernel, dh=dh),
        out_shape=jax.ShapeDtypeStruct((m, xw), jnp.bfloat16),
        grid=(m // tm,),
        in_specs=[pl.BlockSpec((tm, xw), lambda i: (i, 0)),
                  pl.BlockSpec((n_mem, xw), lambda i: (seq_of(i), 0)),
                  pl.BlockSpec((n_mem, xw), lambda i: (seq_of(i), 0))],
        out_specs=pl.BlockSpec((tm, xw), lambda i: (i, 0)),
        compiler_params=_params(blk, ("parallel",)),
        name="memory_attention",
    )(q, k, v)


def kernel(x_prompt, x_sample, mem_prompt, mem_sample, w_in, w_out, g_pre_mix, g_post_mix, i_bias, f_bias, g_mlstm_out, rpb, g_pre_xattn, g_post_xattn, g_mem, wq_x, wk_x, wv_x, wo_x, g_pre_ffn, g_post_ffn, w_gate, w_up, w_down):
    bf16 = jnp.bfloat16
    depth = w_in.shape[0]
    d = x_prompt.shape[-1]
    bp, tp, _ = x_prompt.shape
    bs, ts, _ = x_sample.shape
    n_mem = mem_prompt.shape[1]
    mix_w = w_out.shape[1]
    v_w = mix_w // 2
    dv = v_w // M_HEADS
    dk = dv // 2
    qk_w = M_HEADS * dk
    na_dh = v_w // NA_HEADS
    n_gates = 4 * M_HEADS
    d_ff = w_gate.shape[2]
    assert w_in.shape[2] == 2 * qk_w + 2 * v_w + n_gates + 3 * v_w

    seqs = [(b * tp, tp) for b in range(bp)] + [(bp * tp + b * ts, ts) for b in range(bs)]
    mem =jnp.concatenate([mem_prompt.reshape(bp * n_mem, d), mem_sample.reshape(bs * n_mem, d)], axis=0)

    m_cols = 2 * qk_w + 2 * v_w
    w_main = jnp.concatenate([w_in[:, :, :m_cols], w_in[:, :, m_cols + n_gates:]], axis=2).astype(bf16)
    w_gates = jnp.pad(w_in[:, :, m_cols:m_cols + n_gates], ((0, 0), (0, 0), (0, GATE_LANES - n_gates))).astype(bf16)
    gate_bias = jnp.concatenate([i_bias[:, 0], f_bias[:, 0], i_bias[:, 1], f_bias[:, 1]], axis=1)
    gate_bias = jnp.pad(gate_bias, ((0, 0), (0, GATE_LANES - n_gates))).astype(jnp.float32)
    w_out_m = w_out[:, :v_w].astype(bf16)
    w_out_n = w_out[:, v_w:].astype(bf16)
    wq, wk, wv, wo = (w.astype(bf16) for w in (wq_x, wk_x, wv_x, wo_x))
    wg, wu, wd = (w.astype(bf16) for w in (w_gate, w_up, w_down))
    na_bias = _na_bias_tables(rpb)

    tm = _tile(np.gcd(bp * tp, bs * ts), TOK_TILE)
    tm_q = _tile(tm, QPROJ_TILE)
    srcs, y, g_post = [x_prompt.reshape(bp * tp, d), x_sample.reshape(bs * ts, d)], None, None
    for l in range(depth):
        h, z, gates = norm_ahead_proj(srcs, y, g_post, g_pre_mix[l], (w_main[l], w_gates[l]), "in_proj",
                                      tm, _tile(w_main.shape[2], IN_PROJ_TN), _tile(tm, IN_PROJ_CHUNKS))
        gb = gate_bias[l].reshape(1, GATE_LANES)
        h_fw = mlstm_scan(z, gates, gb, seqs, False, dk, dv)
        y_m = mlstm_scan(z, gates, gb, seqs, True, dk, dv, combine=(h_fw, g_mlstm_out[l]))
        y_n = neighborhood_attention(z, na_bias[l], seqs, m_cols // v_w, na_dh)
        y = matmul([y_m, y_n], [w_out_m[l], w_out_n[l]], jnp.float32)
        h, q = norm_ahead_proj([h], y, g_post_mix[l], g_pre_xattn[l], (wq[l],), "plain",
                               tm_q, wq.shape[2], 1)
        mem_n = rmsnorm_cast(mem, g_mem[l])
        k = matmul([mem_n], [wk[l]], bf16)
        v = matmul([mem_n], [wv[l]], bf16)
        o = memory_attention(q, k, v, seqs, n_mem)
        y = matmul([o], [wo[l]], jnp.float32)
        h, hidden = norm_ahead_proj([h], y, g_post_xattn[l], g_pre_ffn[l], (wg[l], wu[l]), "gate_up",
                                    tm, _tile(d_ff, FF_UP_TN), _tile(tm, FF_UP_CHUNKS))
        y = matmul([hidden], [wd[l]], jnp.float32, tn_pref=FF_DOWN_TILE, tm_pref=FF_DOWN_TILE)
        srcs, g_post = [h], g_post_ffn[l]

    out_p, out_s = residual_split(h, y, g_post, bp * tp)
    return (out_p.reshape(bp, tp, d), out_s.reshape(bs, ts, d))
```

```python
import functools

import jax
import jax.numpy as jnp
import numpy as np
from jax import lax
from jax.experimental import pallas as pl
from jax.experimental.pallas import tpu as pltpu

GRID_W = 64
M_HEADS = 4
GATE_CAP = 15.0
NA_HEADS = 16
NA_KR = 8
NA_KC = 16
X_HEADS = 4
RMS_EPS = 1e-6
GATE_LANES = 128
MASK_NEG = -1e30

V7X_VMEM_BYTES = 64 * 1024 * 1024
V7X_VMEM_BUDGET = 56 * 1024 * 1024
LANE = 128

MLSTM_CHUNK = 256
TOK_TILE = 1024
ELEM_TILE = 256
XATTN_TILE = 512
OUT_PROJ_TM = 512
OUT_PROJ_TN = 1024
OUT_PROJ_CHUNKS = 4
IN_PROJ_TM = 512
IN_PROJ_TN = 1536
IN_PROJ_CHUNKS = 8
FF_UP_TN = 256
FF_UP_CHUNKS = 32
FF_DOWN_TILE = 512


def _params(block_bytes, semantics):
    limit = min(V7X_VMEM_BUDGET, int(2 * block_bytes) + 16 * 1024 * 1024)
    return pltpu.CompilerParams(dimension_semantics=semantics, vmem_limit_bytes=limit)


def _nbytes(shape, dtype):
    return int(np.prod(shape)) * jnp.dtype(dtype).itemsize


def _tile(m, pref):
    t = min(m, pref)
    while m % t:
        t //= 2
    return t


def _col_tile(n, pref):
    t = (min(n, pref) // LANE) * LANE
    while n % t:
        t -= LANE
    return t


def _rms_scale(x):
    return lax.rsqrt(jnp.mean(x * x, axis=-1, keepdims=True) + RMS_EPS)


def _rmsnorm_cast_kernel(x_ref, g_ref, o_ref):
    x = x_ref[...]
    o_ref[...] = (x * _rms_scale(x) * g_ref[...]).astype(o_ref.dtype)


def rmsnorm_cast(x, g):
    m, d = x.shape
    tm = _tile(m, ELEM_TILE)
    blk = _nbytes((tm, d), jnp.float32) + _nbytes((tm, d), jnp.bfloat16)
    return pl.pallas_call(
        _rmsnorm_cast_kernel,
        out_shape=jax.ShapeDtypeStruct((m, d), jnp.bfloat16),
        grid=(m // tm,),
        in_specs=[pl.BlockSpec((tm, d), lambda i: (i, 0)),
                  pl.BlockSpec((1, d), lambda i: (0, 0))],
        out_specs=pl.BlockSpec((tm, d), lambda i: (i, 0)),
        compiler_params=_params(blk, ("parallel",)),
        name="rmsnorm_cast",
    )(x, g.reshape(1, d))


def _residual_split_kernel(h_ref, y_ref, gpost_ref, a_out_ref, b_out_ref, *, n_first):
    y = y_ref[...]
    x = h_ref[...] + y * _rms_scale(y) * gpost_ref[...]
    i = pl.program_id(0)

    @pl.when(i < n_first)
    def _():
        a_out_ref[...] = x

    @pl.when(i >= n_first)
    def _():
        b_out_ref[...] = x


def residual_split(h, y, g_post, rows_first):
    m, d = h.shape
    tm = _tile(np.gcd(rows_first, m - rows_first), ELEM_TILE)
    n_first = rows_first // tm
    row = pl.BlockSpec((tm, d), lambda i: (i, 0))
    vec = pl.BlockSpec((1, d), lambda i: (0, 0))
    blk = 4 * _nbytes((tm, d), jnp.float32)
    return pl.pallas_call(
        functools.partial(_residual_split_kernel, n_first=n_first),
        out_shape=(jax.ShapeDtypeStruct((rows_first, d), jnp.float32),
                   jax.ShapeDtypeStruct((m - rows_first, d), jnp.float32)),
        grid=(m // tm,), in_specs=[row, row, vec],
        out_specs=(pl.BlockSpec((tm, d), lambda i: (jnp.minimum(i, n_first - 1), 0)),
                   pl.BlockSpec((tm, d), lambda i: (jnp.maximum(i - n_first, 0), 0))),
        compiler_params=_params(blk, ("arbitrary",)),
        name="residual_split",
    )(h, y, g_post.reshape(1, d))


def _norm_ahead_kernel(*refs, n_src, src_split, has_y, mode, n_row_tiles, n_chunks, chunk_rows):
    refs = list(refs)
    src_refs = [refs.pop(0) for _ in range(n_src)]
    y_ref, gpost_ref = (refs.pop(0), refs.pop(0)) if has_y else (None, None)
    gpre_ref = refs.pop(0)
    if mode == "in_proj":
        w_ref, wx_ref, h_out_ref, o_ref, ox_ref, a_even, a_odd = refs
    elif mode == "gate_up":
        w_ref, wu_ref, h_out_ref, o_ref, a_even, a_odd = refs
    else:
        w_ref, h_out_ref, o_ref, a_even, a_odd = refs
    r = pl.program_id(0)
    j = pl.program_id(1)

    def norm_chunk(a_dst):
        cj = jnp.minimum(j, n_chunks - 1)
        if n_src == 2:
            x = jnp.where(r * n_chunks + cj < src_split, src_refs[0][...], src_refs[1][...])
        else:
            x = src_refs[0][...]
        if has_y:
            y = y_ref[...]
            x = x + y * _rms_scale(y) * gpost_ref[...]
        h_out_ref[...] = x
        a = (x * _rms_scale(x) * gpre_ref[...]).astype(a_dst.dtype)
        a_dst[pl.ds(pl.multiple_of(cj * chunk_rows, chunk_rows), chunk_rows), :] = a

    def project(a_src):
        a = a_src[...]
        if mode == "gate_up":
            g = jnp.dot(a, w_ref[...], preferred_element_type=jnp.float32)
            u = jnp.dot(a, wu_ref[...], preferred_element_type=jnp.float32)
            o_ref[...] = (g * jax.nn.sigmoid(g) * u).astype(o_ref.dtype)
        else:
            o_ref[...] = jnp.dot(a, w_ref[...], preferred_element_type=jnp.float32).astype(o_ref.dtype)

    @pl.when(r == 0)
    def _():
        norm_chunk(a_even)

    for parity, (a_dst, a_src) in enumerate(((a_even, a_odd), (a_odd, a_even))):
        @pl.when((r > 0) & (r % 2 == parity))
        def _(a_dst=a_dst, a_src=a_src):
            norm_chunk(a_dst)
            project(a_src)

        if mode == "in_proj":
            @pl.when((r > 0) & (r % 2 == parity) & (j == 0))
            def _(a_src=a_src):
                ox_ref[...] = jnp.dot(a_src[...], wx_ref[...], preferred_element_type=jnp.float32)


def norm_ahead_proj(srcs, y, g_post, g_pre, weights, mode, tm, tn, n_chunks):
    d = srcs[0].shape[1]
    m = sum(s.shape[0] for s in srcs)
    n = weights[0].shape[1]
    assert m % tm == 0 and n % tn == 0 and tm % n_chunks == 0
    n_row_tiles, n_col_tiles = m // tm, n // tn
    assert n_chunks <= n_col_tiles
    chunk_rows = tm // n_chunks
    total_chunks = n_row_tiles * n_chunks
    src_split = srcs[0].shape[0] // chunk_rows
    for s in srcs:
        assert s.shape[0] % chunk_rows == 0

    def chunk(r, j):
        return jnp.minimum(r * n_chunks + jnp.minimum(j, n_chunks - 1), total_chunks - 1)

    def col(r, j):
        return jnp.where(r == 0, 0, j)

    row_blk = (chunk_rows, d)
    if len(srcs) == 2:
        in_specs = [pl.BlockSpec(row_blk, lambda r, j: (jnp.minimum(chunk(r, j), src_split - 1), 0)),
                    pl.BlockSpec(row_blk, lambda r, j: (jnp.maximum(chunk(r, j) - src_split, 0), 0))]
    else:
        in_specs = [pl.BlockSpec(row_blk, lambda r, j: (chunk(r, j), 0))]
    args = list(srcs)
    vec = pl.BlockSpec((1, d), lambda r, j: (0, 0))
    if y is not None:
        in_specs += [pl.BlockSpec(row_blk, lambda r, j: (chunk(r, j), 0)), vec]
        args += [y, g_post.reshape(1, d)]
    in_specs.append(vec)
    args.append(g_pre.reshape(1, d))
    w_spec = pl.BlockSpec((d, tn), lambda r, j: (0, col(r, j)))
    out_spec = pl.BlockSpec((tm, tn), lambda r, j: (jnp.maximum(r - 1, 0), col(r, j)))
    out_shape = [jax.ShapeDtypeStruct((m, d), jnp.float32), jax.ShapeDtypeStruct((m, n), jnp.bfloat16)]
    out_specs = [pl.BlockSpec(row_blk, lambda r, j: (chunk(r, j), 0)), out_spec]
    blk = ((len(srcs) + (y is not None) + 1) * _nbytes(row_blk, jnp.float32) + _nbytes((d, tn), jnp.bfloat16)
           + _nbytes((tm, tn), jnp.bfloat16) + 2 * _nbytes((tm, tn), jnp.float32) + _nbytes((tm, d), jnp.bfloat16))
    if mode == "in_proj":
        nx = weights[1].shape[1]
        in_specs += [w_spec, pl.BlockSpec((d, nx), lambda r, j: (0, 0))]
        out_shape.append(jax.ShapeDtypeStruct((m, nx), jnp.float32))
        out_specs.append(pl.BlockSpec((tm, nx), lambda r, j: (jnp.maximum(r - 1, 0), 0)))
        blk += _nbytes((d, nx), jnp.bfloat16) + _nbytes((tm, nx), jnp.float32)
    elif mode == "gate_up":
        in_specs += [w_spec, w_spec]
        blk += _nbytes((d, tn), jnp.bfloat16) + 2 * _nbytes((tm, tn), jnp.float32)
    else:
        in_specs.append(w_spec)
    args += list(weights)
    return pl.pallas_call(
        functools.partial(_norm_ahead_kernel, n_src=len(srcs), src_split=src_split, has_y=y is not None,
                          mode=mode, n_row_tiles=n_row_tiles, n_chunks=n_chunks, chunk_rows=chunk_rows),
        out_shape=tuple(out_shape),
        grid=(n_row_tiles + 1, n_col_tiles),
        in_specs=in_specs,
        out_specs=tuple(out_specs),
        scratch_shapes=[pltpu.VMEM((tm, d), jnp.bfloat16), pltpu.VMEM((tm, d), jnp.bfloat16)],
        compiler_params=_params(blk, ("arbitrary", "arbitrary")),
        name="norm_ahead_" + mode,
    )(*args)


def _norm_behind_kernel(*refs, n_pairs, n_row_tiles, n_col_tiles, n_chunks, chunk_rows):
    a_refs = refs[:n_pairs]
    w_refs = refs[n_pairs:2 * n_pairs]
    h_ref, gpost_ref, gpre_ref, h_out_ref, a_out_ref, y_even, y_odd = refs[2 * n_pairs:]
    r = pl.program_id(0)
    j = pl.program_id(1)
    tn = w_refs[0].shape[1]
    d = h_ref.shape[1]

    def multiply(y_dst):
        acc = jnp.dot(a_refs[0][...], w_refs[0][...], preferred_element_type=jnp.float32)
        for p in range(1, n_pairs):
            acc += jnp.dot(a_refs[p][...], w_refs[p][...], preferred_element_type=jnp.float32)
        y_dst[j] = acc

    def norm_chunk(y_src):
        rows = pl.ds(pl.multiple_of(jnp.minimum(j, n_chunks - 1) * chunk_rows, chunk_rows), chunk_rows)
        cols = [slice(c * tn, (c + 1) * tn) for c in range(n_col_tiles)]
        y = [y_src[c, rows, :] for c in range(n_col_tiles)]
        y_scale = lax.rsqrt(sum(jnp.sum(t * t, axis=-1, keepdims=True) for t in y) / d + RMS_EPS)
        x = [h_ref[:, cols[c]] + y[c] * y_scale * gpost_ref[:, cols[c]] for c in range(n_col_tiles)]
        x_scale = lax.rsqrt(sum(jnp.sum(t * t, axis=-1, keepdims=True) for t in x) / d + RMS_EPS)
        for c in range(n_col_tiles):
            h_out_ref[:, cols[c]] = x[c]
            a_out_ref[:, cols[c]] = (x[c] * x_scale * gpre_ref[:, cols[c]]).astype(a_out_ref.dtype)

    bufs = (y_even, y_odd)

    @pl.when(r == 0)
    def _():
        multiply(y_even)

    for parity in range(2):
        @pl.when((r > 0) & (r < n_row_tiles) & (r % 2 == parity))
        def _(parity=parity):
            multiply(bufs[parity])
            norm_chunk(bufs[1 - parity])

    @pl.when(r == n_row_tiles)
    def _():
        norm_chunk(bufs[(n_row_tiles - 1) % 2])


def matmul_norm_behind(a_list, w_list, h, g_post, g_pre, tm, tn, n_chunks):
    m, d = h.shape
    assert w_list[0].shape[1] == d and m % tm == 0 and d % tn == 0 and tm % n_chunks == 0
    n_row_tiles, n_col_tiles = m // tm, d // tn
    assert n_chunks <= n_col_tiles
    chunk_rows = tm // n_chunks
    total_chunks = n_row_tiles * n_chunks

    def row(r, j):
        return jnp.minimum(r, n_row_tiles - 1)

    def col(r, j):
        return jnp.where(r == n_row_tiles, 0, j)

    def chunk(r, j):
        return jnp.clip((r - 1) * n_chunks + jnp.minimum(j, n_chunks - 1), 0, total_chunks - 1)

    in_specs, blk = [], 0
    for a in a_list:
        in_specs.append(pl.BlockSpec((tm, a.shape[1]), lambda r, j: (row(r, j), 0)))
        blk += _nbytes((tm, a.shape[1]), a.dtype)
    for w in w_list:
        in_specs.append(pl.BlockSpec((w.shape[0], tn), lambda r, j: (0, col(r, j))))
        blk += _nbytes((w.shape[0], tn), w.dtype)
    chunk_spec = pl.BlockSpec((chunk_rows, d), lambda r, j: (chunk(r, j), 0))
    vec = pl.BlockSpec((1, d), lambda r, j: (0, 0))
    in_specs += [chunk_spec, vec, vec]
    blk += (2 * _nbytes((chunk_rows, d), jnp.float32) + _nbytes((chunk_rows, d), jnp.bfloat16)
            + _nbytes((tm, d), jnp.float32) + _nbytes((tm, tn), jnp.float32))
    return pl.pallas_call(
        functools.partial(_norm_behind_kernel, n_pairs=len(a_list), n_row_tiles=n_row_tiles,
                          n_col_tiles=n_col_tiles, n_chunks=n_chunks, chunk_rows=chunk_rows),
        out_shape=(jax.ShapeDtypeStruct((m, d), jnp.float32), jax.ShapeDtypeStruct((m, d), jnp.bfloat16)),
        grid=(n_row_tiles + 1, n_col_tiles),
        in_specs=in_specs,
        out_specs=(chunk_spec, chunk_spec),
        scratch_shapes=[pltpu.VMEM((n_col_tiles, tm, tn), jnp.float32),
                        pltpu.VMEM((n_col_tiles, tm, tn), jnp.float32)],
        compiler_params=_params(blk, ("arbitrary", "arbitrary")),
        name="matmul_norm_behind",
    )(*a_list, *w_list, h, g_post.reshape(1, d), g_pre.reshape(1, d))


def _matmul_kernel(*refs, n_pairs):
    o_ref = refs[-1]
    acc = jnp.dot(refs[0][...], refs[n_pairs][...], preferred_element_type=jnp.float32)
    for p in range(1, n_pairs):
        acc += jnp.dot(refs[p][...], refs[n_pairs + p][...], preferred_element_type=jnp.float32)
    o_ref[...] = acc.astype(o_ref.dtype)


def matmul(a_list, w_list, out_dtype, tn_pref=1024, tm_pref=TOK_TILE):
    m = a_list[0].shape[0]
    n = w_list[0].shape[1]
    tm = _tile(m, tm_pref)
    tn = _tile(n, tn_pref)
    in_specs, blk = [], 0
    for a in a_list:
        k = a.shape[1]
        in_specs.append(pl.BlockSpec((tm, k), lambda i, j: (i, 0)))
        blk += _nbytes((tm, k), a.dtype)
    for w in w_list:
        k = w.shape[0]
        in_specs.append(pl.BlockSpec((k, tn), lambda i, j: (0, j)))
        blk += _nbytes((k, tn), w.dtype)
    blk += _nbytes((tm, tn), out_dtype) + _nbytes((tm, tn), jnp.float32)
    return pl.pallas_call(
        functools.partial(_matmul_kernel, n_pairs=len(a_list)),
        out_shape=jax.ShapeDtypeStruct((m, n), out_dtype),
        grid=(m // tm, n // tn),
        in_specs=in_specs,
        out_specs=pl.BlockSpec((tm, tn), lambda i, j: (i, j)),
        compiler_params=_params(blk, ("parallel", "arbitrary")),
        name="matmul",
    )(*a_list, *w_list)


def _log_sigmoid(x):
    return jnp.minimum(x, 0.0) - jnp.log(1.0 + jnp.exp(-jnp.abs(x)))


def _split3_dot(tri, x):
    hi = x.astype(jnp.bfloat16)
    r1 = x - hi.astype(jnp.float32)
    mid = r1.astype(jnp.bfloat16)
    lo = (r1 - mid.astype(jnp.float32)).astype(jnp.bfloat16)
    out = jnp.dot(tri, hi, preferred_element_type=jnp.float32)
    out += jnp.dot(tri, mid, preferred_element_type=jnp.float32)
    out += jnp.dot(tri, lo, preferred_element_type=jnp.float32)
    return out


def _mlstm_kernel(*refs, reverse, n_chunks, seq_first_chunks, dk, dv, combine):
    if combine:
        q_ref, k_ref, v_ref, g_ref, gb_ref, hf_ref, og_ref, gout_ref, o_ref, c_scr, n_scr, m_scr = refs
    else:
        q_ref, k_ref, v_ref, g_ref, gb_ref, o_ref, c_scr, n_scr, m_scr = refs
    L = q_ref.shape[0]
    step = pl.program_id(0)
    chunk = (n_chunks - 1 - step) if reverse else step

    is_start = functools.reduce(jnp.logical_or, [chunk == c for c in seq_first_chunks])

    @pl.when(is_start)
    def _():
        c_scr[...] = jnp.zeros_like(c_scr)
        n_scr[...] = jnp.zeros_like(n_scr)
        m_scr[...] = jnp.zeros_like(m_scr)

    rows = lax.broadcasted_iota(jnp.int32, (L, L), 0)
    cols = lax.broadcasted_iota(jnp.int32, (L, L), 1)
    keep = (cols >= rows) if reverse else (cols <= rows)
    tri = jnp.where(keep, 1.0, 0.0).astype(jnp.bfloat16)

    pre = g_ref[...] + gb_ref[...]
    capped = GATE_CAP * jnp.tanh(pre / GATE_CAP)
    log_f = _log_sigmoid(capped)
    cum = _split3_dot(tri, log_f)
    capped_t = capped.T
    cum_t = cum.T
    edge = 0 if reverse else L - 1
    d = 1 if reverse else 0
    heads = range(M_HEADS)

    q, k, v, c_prev, s, qc = [], [], [], [], [], []
    for h in heads:
        q.append(q_ref[:, h * dk:(h + 1) * dk])
        k.append(k_ref[:, h * dk:(h + 1) * dk] * jnp.asarray(dk ** -0.5, k_ref.dtype))
        v.append(v_ref[:, h * dv:(h + 1) * dv])
        c_prev.append(c_scr[h])
        s.append(lax.dot_general(q[h], k[h], (((1,), (1,)), ((), ())), preferred_element_type=jnp.float32))
        qc.append(jnp.dot(q[h], c_prev[h].astype(jnp.bfloat16), preferred_element_type=jnp.float32))

    p, w_inter, den, m_row = [], [], [], []
    for h in heads:
        il = (2 * d) * M_HEADS + h
        fl = (2 * d + 1) * M_HEADS + h
        b_col = cum[:, fl:fl + 1]
        li_col = capped[:, il:il + 1]
        total = cum[edge:edge + 1, fl:fl + 1]
        src_row = capped_t[il:il + 1, :] - cum_t[fl:fl + 1, :]
        m_prev = m_scr[h, 0:1, 0:1]
        n_prev = n_scr[h, 0:1, :]

        d_log = jnp.where(keep, b_col + src_row, MASK_NEG)
        m_inter = b_col + m_prev
        m_row.append(jnp.maximum(m_inter, jnp.max(d_log, axis=1, keepdims=True)))
        ph = s[h] * jnp.exp(d_log - m_row[h])
        w_inter.append(jnp.exp(m_inter - m_row[h]))
        den.append(w_inter[h] * jnp.sum(q[h].astype(jnp.float32) * n_prev, axis=1, keepdims=True)
                   + jnp.sum(ph, axis=1, keepdims=True))
        p.append(ph.astype(jnp.bfloat16))

        g_col = total - b_col + li_col
        m_new = jnp.maximum(total + m_prev, jnp.max(g_col, axis=0, keepdims=True))
        w_tok = jnp.exp(g_col - m_new)
        decay = jnp.exp(total + m_prev - m_new)
        kw = k[h].astype(jnp.float32) * w_tok
        c_scr[h] = decay * c_prev[h] + lax.dot_general(
            kw.astype(jnp.bfloat16), v[h], (((0,), (0,)), ((), ())), preferred_element_type=jnp.float32)
        n_scr[h] = jnp.broadcast_to(decay * n_prev + jnp.sum(kw, axis=0, keepdims=True), n_scr.shape[1:])
        m_scr[h] = jnp.broadcast_to(m_new, m_scr.shape[1:])

    for h in heads:
        sl = slice(h * dv, (h + 1) * dv)
        num = w_inter[h] * qc[h] + jnp.dot(p[h], v[h], preferred_element_type=jnp.float32)
        x = num / jnp.maximum(jnp.abs(den[h]), jnp.exp(-m_row[h]))
        if combine:
            x = x + hf_ref[:, sl]
            x = x * _rms_scale(x) * gout_ref[:, sl]
            x = x * jax.nn.sigmoid(og_ref[:, sl].astype(jnp.float32))
        o_ref[:, sl] = x.astype(o_ref.dtype)


def mlstm_scan(z, gates, gate_bias, seqs, reverse, dk, dv, combine=None):
    m = z.shape[0]
    L = MLSTM_CHUNK
    qw, vw = M_HEADS * dk, M_HEADS * dv
    n_chunks = m // L
    for s0, t in seqs:
        assert s0 % L == 0 and t % L == 0
    if reverse:
        firsts = tuple((s0 + t) // L - 1 for s0, t in seqs)
    else:
        firsts = tuple(s0 // L for s0, _ in seqs)
    cidx = (lambda i: n_chunks - 1 - i) if reverse else (lambda i: i)
    assert (2 * qw) % vw == 0
    v_blk = (2 * qw) // vw
    in_specs = [pl.BlockSpec((L, qw), lambda i: (cidx(i), 0)),
                pl.BlockSpec((L, qw), lambda i: (cidx(i), 1)),
                pl.BlockSpec((L, vw), lambda i: (cidx(i), v_blk)),
                pl.BlockSpec((L, GATE_LANES), lambda i: (cidx(i), 0)),
                pl.BlockSpec((1, GATE_LANES), lambda i: (0, 0))]
    args = [z, z, z, gates, gate_bias]
    blk = (2 * _nbytes((L, qw), z.dtype) + _nbytes((L, vw), z.dtype) + _nbytes((L, GATE_LANES), jnp.float32)
           + _nbytes((L, vw), jnp.float32) + _nbytes((M_HEADS, dk, dv), jnp.float32))
    out_dtype = jnp.float32
    if combine is not None:
        h_other, g_out = combine
        in_specs += [pl.BlockSpec((L, vw), lambda i: (cidx(i), 0)),
                     pl.BlockSpec((L, vw), lambda i: (cidx(i), v_blk + 1)),
                     pl.BlockSpec((1, vw), lambda i: (0, 0))]
        args += [h_other, z, g_out.reshape(1, vw)]
        blk += _nbytes((L, vw), jnp.float32) + _nbytes((L, vw), z.dtype)
        out_dtype = jnp.bfloat16
    scratch = [pltpu.VMEM((M_HEADS, dk, dv), jnp.float32),
               pltpu.VMEM((M_HEADS, 8, dk), jnp.float32),
               pltpu.VMEM((M_HEADS, 8, LANE), jnp.float32)]
    return pl.pallas_call(
        functools.partial(_mlstm_kernel, reverse=reverse, n_chunks=n_chunks,
                          seq_first_chunks=firsts, dk=dk, dv=dv, combine=combine is not None),
        out_shape=jax.ShapeDtypeStruct((m, vw), out_dtype),
        grid=(n_chunks,),
        in_specs=in_specs,
        out_specs=pl.BlockSpec((L, vw), lambda i: (cidx(i), 0)),
        scratch_shapes=scratch,
        compiler_params=_params(blk, ("arbitrary",)),
        name="mlstm_bwd" if reverse else "mlstm_fwd",
    )(*args)


NA_PAIR = 2
NA_WIN = NA_KR + NA_PAIR - 1
NA_CASES = ((0, 0, 0), (2, 0, 0), (4, 0, 1), (5, 1, 1), (7, 1, 1))
NA_HEAD_GROUP = 4
LOG2E = float(np.log2(np.e))


def _na_kernel(q_ref, k_ref, v_ref, bias_ref, o_ref, *, dh):
    scale2 = dh ** -0.5 * LOG2E
    for h0 in range(0, NA_HEADS, NA_HEAD_GROUP):
        heads = range(h0, h0 + NA_HEAD_GROUP)
        s = {}
        for h in heads:
            sl = slice(h * dh, (h + 1) * dh)
            s[h] = lax.dot_general(q_ref[:, sl], k_ref[:, sl], (((1,), (1,)), ((), ())),
                                   preferred_element_type=jnp.float32)
        e, inv = {}, {}
        for h in heads:
            t = s[h] * scale2 + bias_ref[0, h]
            eh = jnp.exp2(t - jnp.max(t, axis=1, keepdims=True))
            inv[h] = 1.0 / jnp.sum(eh, axis=1, keepdims=True)
            e[h] = eh.astype(jnp.bfloat16)
        for h in heads:
            sl = slice(h * dh, (h + 1) * dh)
            o = jnp.dot(e[h], v_ref[:, sl], preferred_element_type=jnp.float32) * inv[h]
            o_ref[:, sl] = o.astype(o_ref.dtype)


def _na_bias_tables(rpb):
    depth = rpb.shape[0]
    qs = np.arange(GRID_W)
    cs = np.clip(qs - NA_KC // 2, 0, GRID_W - NA_KC)
    ccol = np.arange(GRID_W)
    col_ok = (ccol[None, :] >= cs[:, None]) & (ccol[None, :] < cs[:, None] + NA_KC)
    dc = ccol[None, :] - qs[:, None] + NA_KC - 1
    n_dr, n_dc = 2 * NA_KR - 1, 2 * NA_KC - 1
    col_sel = (dc[None] == np.arange(n_dc)[:, None, None]) & col_ok[None]
    planes = jnp.dot(rpb.astype(jnp.float32).reshape(-1, n_dc),
                     col_sel.reshape(n_dc, GRID_W * GRID_W).astype(jnp.float32),
                     precision=lax.Precision.HIGHEST)
    planes = planes.reshape(depth, NA_HEADS, n_dr, GRID_W, GRID_W)
    planes = jnp.where(col_ok, planes, MASK_NEG) * LOG2E
    planes = jnp.transpose(planes, (0, 1, 3, 2, 4))
    cases = []
    for rel0, *offs in NA_CASES:
        members = []
        for j in range(NA_PAIR):
            dr0 = offs[j] - rel0 - j + NA_KR - 1
            rows = jnp.pad(planes[:, :, :, dr0:dr0 + NA_KR], ((0, 0),) * 3 + ((offs[j], NA_WIN - NA_KR - offs[j]), (0, 0)),
                           constant_values=MASK_NEG * LOG2E)
            members.append(rows.reshape(depth, NA_HEADS, GRID_W, NA_WIN * GRID_W))
        cases.append(jnp.stack(members, axis=2))
    t = jnp.stack(cases, axis=1)
    return t.reshape(depth, len(NA_CASES), NA_HEADS, NA_PAIR * GRID_W, NA_WIN * GRID_W)


def neighborhood_attention(z, bias, seqs, q_col_block, dh):
    m = z.shape[0]
    hw = NA_HEADS * dh
    n_rows = m // GRID_W
    bounds = [(s0 // GRID_W, t // GRID_W) for s0, t in seqs]
    for r0, nr in bounds:
        assert nr % NA_PAIR == 0 and r0 % NA_PAIR == 0 and nr >= NA_KR + 4

    def geometry(g):
        r = g * NA_PAIR
        lo = jnp.int32(0)
        nr_ = jnp.int32(0)
        for r0, nr in bounds:
            inside = (r >= r0) & (r < r0 + nr)
            lo = jnp.where(inside, r0, lo)
            nr_ = jnp.where(inside, nr, nr_)
        top = r - lo
        bot = lo + nr_ - NA_PAIR - r
        u = jnp.clip(r - NA_KR // 2, lo, lo + nr_ - NA_WIN)
        case = jnp.where(top == 0, 0, jnp.where(top == 2, 1, jnp.where(bot == 0, 4, jnp.where(bot == 2, 3, 2))))
        return u, case

    tq, tk = NA_PAIR * GRID_W, NA_WIN * GRID_W

    def kv_spec(col_block):
        return pl.BlockSpec((pl.Element(tk), pl.Element(hw)),
                            lambda g: (geometry(g)[0] * GRID_W, col_block * hw))

    in_specs = [pl.BlockSpec((tq, hw), lambda g: (g, q_col_block)),
                kv_spec(q_col_block + 1), kv_spec(q_col_block + 2),
                pl.BlockSpec((1, NA_HEADS, tq, tk), lambda g: (geometry(g)[1], 0, 0, 0))]
    blk = (2 * _nbytes((tq, hw), z.dtype) + 2 * _nbytes((tk, hw), z.dtype)
           + 2 * _nbytes((NA_HEADS, tq, tk), jnp.float32))
    return pl.pallas_call(
        functools.partial(_na_kernel, dh=dh),
        out_shape=jax.ShapeDtypeStruct((m, hw), jnp.bfloat16),
        grid=(n_rows // NA_PAIR,),
        in_specs=in_specs,
        out_specs=pl.BlockSpec((tq, hw), lambda g: (g, 0)),
        compiler_params=_params(blk, ("parallel",)),
        name="neighborhood_attention",
    )(z, z, z, bias)


def _xattn_kernel(q_ref, k_ref, v_ref, o_ref, *, dh):
    scale = dh ** -0.5
    for h in range(X_HEADS):
        sl = slice(h * dh, (h + 1) * dh)
        s = lax.dot_general(q_ref[:, sl], k_ref[:, sl], (((1,), (1,)), ((), ())),
                            preferred_element_type=jnp.float32) * scale
        s = s - jnp.max(s, axis=1, keepdims=True)
        e = jnp.exp(s)
        p = e / jnp.sum(e, axis=1, keepdims=True)
        o_ref[:, sl] = jnp.dot(p.astype(v_ref.dtype), v_ref[:, sl],
                               preferred_element_type=jnp.float32).astype(o_ref.dtype)


def memory_attention(q, k, v, seqs, n_mem):
    m, xw = q.shape
    dh = xw // X_HEADS
    tm = XATTN_TILE
    for s0, t in seqs:
        tm = min(tm, _tile(t, tm))
    starts = [s0 // tm for s0, _ in seqs]
    for s0, _ in seqs:
        assert s0 % tm == 0

    def seq_of(i):
        s = jnp.int32(0)
        for idx, st in enumerate(starts):
            s = jnp.where(i >= st, idx, s)
        return s

    blk = 2 * _nbytes((tm, xw), q.dtype) + 2 * _nbytes((n_mem, xw), k.dtype) + 2 * _nbytes((tm, n_mem), jnp.float32)
    return pl.pallas_call(
        functools.partial(_xattn_kernel, dh=dh),
        out_shape=jax.ShapeDtypeStruct((m, xw), jnp.bfloat16),
        grid=(m // tm,),
        in_specs=[pl.BlockSpec((tm, xw), lambda i: (i, 0)),
                  pl.BlockSpec((n_mem, xw), lambda i: (seq_of(i), 0)),
                  pl.BlockSpec((n_mem, xw), lambda i: (seq_of(i), 0))],
        out_specs=pl.BlockSpec((tm, xw), lambda i: (i, 0)),
        compiler_params=_params(blk, ("parallel",)),
        name="memory_attention",
    )(q, k, v)


def kernel(x_prompt, x_sample, mem_prompt, mem_sample, w_in, w_out, g_pre_mix, g_post_mix, i_bias, f_bias, g_mlstm_out, rpb, g_pre_xattn, g_post_xattn, g_mem, wq_x, wk_x, wv_x, wo_x, g_pre_ffn, g_post_ffn, w_gate, w_up, w_down):
    bf16 = jnp.bfloat16
    depth = w_in.shape[0]
    d = x_prompt.shape[-1]
    bp, tp, _ = x_prompt.shape
    bs, ts, _ = x_sample.shape
    n_mem = mem_prompt.shape[1]
    mix_w = w_out.shape[1]
    v_w = mix_w // 2
    dv = v_w // M_HEADS
    dk = dv // 2
    qk_w = M_HEADS * dk
    na_dh = v_w // NA_HEADS
    n_gates = 4 * M_HEADS
    d_ff = w_gate.shape[2]
    assert w_in.shape[2] == 2 * qk_w + 2 * v_w + n_gates + 3 * v_w

    seqs = [(b * tp, tp) for b in range(bp)] + [(bp * tp + b * ts, ts) for b in range(bs)]
    mem =jnp.concatenate([mem_prompt.reshape(bp * n_mem, d), mem_sample.reshape(bs * n_mem, d)], axis=0)

    m_cols = 2 * qk_w + 2 * v_w
    w_main = jnp.concatenate([w_in[:, :, :m_cols], w_in[:, :, m_cols + n_gates:]], axis=2).astype(bf16)
    w_gates = jnp.pad(w_in[:, :, m_cols:m_cols + n_gates], ((0, 0), (0, 0), (0, GATE_LANES - n_gates))).astype(bf16)
    gate_bias = jnp.concatenate([i_bias[:, 0], f_bias[:, 0], i_bias[:, 1], f_bias[:, 1]], axis=1)
    gate_bias = jnp.pad(gate_bias, ((0, 0), (0, GATE_LANES - n_gates))).astype(jnp.float32)
    w_out_m = w_out[:, :v_w].astype(bf16)
    w_out_n = w_out[:, v_w:].astype(bf16)
    wq, wk, wv, wo = (w.astype(bf16) for w in (wq_x, wk_x, wv_x, wo_x))
    wg, wu, wd = (w.astype(bf16) for w in (w_gate, w_up, w_down))
    na_bias = _na_bias_tables(rpb)

    tm = _tile(np.gcd(bp * tp, bs * ts), TOK_TILE)
    srcs, y, g_post = [x_prompt.reshape(bp * tp, d), x_sample.reshape(bs * ts, d)], None, None
    for l in range(depth):
        tm_in = _tile(tm, IN_PROJ_TM)
        h, z, gates = norm_ahead_proj(srcs, y, g_post, g_pre_mix[l], (w_main[l], w_gates[l]), "in_proj",
                                      tm_in, _col_tile(w_main.shape[2], IN_PROJ_TN), _tile(tm_in, IN_PROJ_CHUNKS))
        gb = gate_bias[l].reshape(1, GATE_LANES)
        h_fw = mlstm_scan(z, gates, gb, seqs, False, dk, dv)
        y_m = mlstm_scan(z, gates, gb, seqs, True, dk, dv, combine=(h_fw, g_mlstm_out[l]))
        y_n = neighborhood_attention(z, na_bias[l], seqs, m_cols // v_w, na_dh)
        tm_out = _tile(tm, OUT_PROJ_TM)
        tn_out = _col_tile(d, OUT_PROJ_TN)
        h, a = matmul_norm_behind([y_m, y_n], [w_out_m[l], w_out_n[l]], h, g_post_mix[l], g_pre_xattn[l],
                                  tm_out, tn_out, min(d // tn_out, OUT_PROJ_CHUNKS))
        q = matmul([a], [wq[l]], bf16)
        mem_n = rmsnorm_cast(mem, g_mem[l])
        k = matmul([mem_n], [wk[l]], bf16)
        v = matmul([mem_n], [wv[l]], bf16)
        o = memory_attention(q, k, v, seqs, n_mem)
        y = matmul([o], [wo[l]], jnp.float32)
        h, hidden = norm_ahead_proj([h], y, g_post_xattn[l], g_pre_ffn[l], (wg[l], wu[l]), "gate_up",
                                    tm, _tile(d_ff, FF_UP_TN), _tile(tm, FF_UP_CHUNKS))
        y = matmul([hidden], [wd[l]], jnp.float32, tn_pref=FF_DOWN_TILE, tm_pref=FF_DOWN_TILE)
        srcs, g_post = [h], g_post_ffn[l]

    out_p, out_s = residual_split(h, y, g_post, bp * tp)
    return (out_p.reshape(bp, tp, d), out_s.reshape(bs, ts, d))
```

```python
import functools

import jax
import jax.numpy as jnp
import numpy as np
from jax import lax
from jax.experimental import pallas as pl
from jax.experimental.pallas import tpu as pltpu

GRID_W = 64
M_HEADS = 4
GATE_CAP = 15.0
NA_HEADS = 16
NA_KR = 8
NA_KC = 16
X_HEADS = 4
RMS_EPS = 1e-6
GATE_LANES = 128
MASK_NEG = -1e30

V7X_VMEM_BYTES = 64 * 1024 * 1024
V7X_VMEM_BUDGET = 56 * 1024 * 1024
LANE = 128

MLSTM_CHUNK = 256
TOK_TILE = 1024
ELEM_TILE = 256
XATTN_TILE = 512
OUT_PROJ_TM = 512
OUT_PROJ_TN = 1024
OUT_PROJ_CHUNKS = 4
IN_PROJ_TM = 512
IN_PROJ_TN = 1536
IN_PROJ_CHUNKS = 8
FF_UP_TN = 256
FF_UP_CHUNKS = 32
FF_DOWN_TILE = 512


def _params(block_bytes, semantics):
    limit = min(V7X_VMEM_BUDGET, int(2 * block_bytes) + 16 * 1024 * 1024)
    return pltpu.CompilerParams(dimension_semantics=semantics, vmem_limit_bytes=limit)


def _nbytes(shape, dtype):
    return int(np.prod(shape)) * jnp.dtype(dtype).itemsize


def _tile(m, pref):
    t = min(m, pref)
    while m % t:
        t //= 2
    return t


def _col_tile(n, pref):
    t = (min(n, pref) // LANE) * LANE
    while n % t:
        t -= LANE
    return t


def _weight_block(w, k, tn, col_index):
    if isinstance(w, tuple):
        arr, layer, kb = w
        return arr, pl.BlockSpec((None, k, tn), lambda *g: (layer, kb, col_index(*g))), arr.shape[2]
    return w, pl.BlockSpec((k, tn), lambda *g: (0, col_index(*g))), w.shape[1]


def _rms_scale(x):
    return lax.rsqrt(jnp.mean(x * x, axis=-1, keepdims=True) + RMS_EPS)


def _rmsnorm_cast_kernel(x_ref, g_ref, o_ref):
    x = x_ref[...]
    o_ref[...] = (x * _rms_scale(x) * g_ref[...]).astype(o_ref.dtype)


def rmsnorm_cast(x, g):
    m, d = x.shape
    tm = _tile(m, ELEM_TILE)
    blk = _nbytes((tm, d), jnp.float32) + _nbytes((tm, d), jnp.bfloat16)
    return pl.pallas_call(
        _rmsnorm_cast_kernel,
        out_shape=jax.ShapeDtypeStruct((m, d), jnp.bfloat16),
        grid=(m // tm,),
        in_specs=[pl.BlockSpec((tm, d), lambda i: (i, 0)),
                  pl.BlockSpec((1, d), lambda i: (0, 0))],
        out_specs=pl.BlockSpec((tm, d), lambda i: (i, 0)),
        compiler_params=_params(blk, ("parallel",)),
        name="rmsnorm_cast",
    )(x, g.reshape(1, d))


def _residual_split_kernel(h_ref, y_ref, gpost_ref, a_out_ref, b_out_ref, *, n_first):
    y = y_ref[...]
    x = h_ref[...] + y * _rms_scale(y) * gpost_ref[...]
    i = pl.program_id(0)

    @pl.when(i < n_first)
    def _():
        a_out_ref[...] = x

    @pl.when(i >= n_first)
    def _():
        b_out_ref[...] = x


def residual_split(h, y, g_post, rows_first):
    m, d = h.shape
    tm = _tile(np.gcd(rows_first, m - rows_first), ELEM_TILE)
    n_first = rows_first // tm
    row = pl.BlockSpec((tm, d), lambda i: (i, 0))
    vec = pl.BlockSpec((1, d), lambda i: (0, 0))
    blk = 4 * _nbytes((tm, d), jnp.float32)
    return pl.pallas_call(
        functools.partial(_residual_split_kernel, n_first=n_first),
        out_shape=(jax.ShapeDtypeStruct((rows_first, d), jnp.float32),
                   jax.ShapeDtypeStruct((m - rows_first, d), jnp.float32)),
        grid=(m // tm,), in_specs=[row, row, vec],
        out_specs=(pl.BlockSpec((tm, d), lambda i: (jnp.minimum(i, n_first - 1), 0)),
                   pl.BlockSpec((tm, d), lambda i: (jnp.maximum(i - n_first, 0), 0))),
        compiler_params=_params(blk, ("arbitrary",)),
        name="residual_split",
    )(h, y, g_post.reshape(1, d))


def _norm_ahead_kernel(*refs, n_src, src_split, has_y, mode, n_row_tiles, n_chunks, chunk_rows):
    refs = list(refs)
    src_refs = [refs.pop(0) for _ in range(n_src)]
    y_ref, gpost_ref = (refs.pop(0), refs.pop(0)) if has_y else (None, None)
    gpre_ref = refs.pop(0)
    if mode == "in_proj":
        w_ref, wx_ref, h_out_ref, o_ref, ox_ref, a_even, a_odd = refs
    elif mode == "gate_up":
        w_ref, wu_ref, h_out_ref, o_ref, a_even, a_odd = refs
    else:
        w_ref, h_out_ref, o_ref, a_even, a_odd = refs
    r = pl.program_id(0)
    j = pl.program_id(1)

    def norm_chunk(a_dst):
        cj = jnp.minimum(j, n_chunks - 1)
        if n_src == 2:
            x = jnp.where(r * n_chunks + cj < src_split, src_refs[0][...], src_refs[1][...])
        else:
            x = src_refs[0][...]
        if has_y:
            y = y_ref[...]
            x = x + y * _rms_scale(y) * gpost_ref[...]
        h_out_ref[...] = x
        a = (x * _rms_scale(x) * gpre_ref[...]).astype(a_dst.dtype)
        a_dst[pl.ds(pl.multiple_of(cj * chunk_rows, chunk_rows), chunk_rows), :] = a

    def project(a_src):
        a = a_src[...]
        if mode == "gate_up":
            g = jnp.dot(a, w_ref[...], preferred_element_type=jnp.float32)
            u = jnp.dot(a, wu_ref[...], preferred_element_type=jnp.float32)
            o_ref[...] = (g * jax.nn.sigmoid(g) * u).astype(o_ref.dtype)
        else:
            o_ref[...] = jnp.dot(a, w_ref[...], preferred_element_type=jnp.float32).astype(o_ref.dtype)

    @pl.when(r == 0)
    def _():
        norm_chunk(a_even)

    for parity, (a_dst, a_src) in enumerate(((a_even, a_odd), (a_odd, a_even))):
        @pl.when((r > 0) & (r % 2 == parity))
        def _(a_dst=a_dst, a_src=a_src):
            norm_chunk(a_dst)
            project(a_src)

        if mode == "in_proj":
            @pl.when((r > 0) & (r % 2 == parity) & (j == 0))
            def _(a_src=a_src):
                ox_ref[...] = jnp.dot(a_src[...], wx_ref[...], preferred_element_type=jnp.float32)


def norm_ahead_proj(srcs, y, g_post, g_pre, weights, mode, tm, tn, n_chunks):
    d = srcs[0].shape[1]
    m = sum(s.shape[0] for s in srcs)
    n = weights[0][0].shape[2] if isinstance(weights[0], tuple) else weights[0].shape[1]
    assert m % tm == 0 and n % tn == 0 and tm % n_chunks == 0
    n_row_tiles, n_col_tiles = m // tm, n // tn
    assert n_chunks <= n_col_tiles
    chunk_rows = tm // n_chunks
    total_chunks = n_row_tiles * n_chunks
    src_split = srcs[0].shape[0] // chunk_rows
    for s in srcs:
        assert s.shape[0] % chunk_rows == 0

    def chunk(r, j):
        return jnp.minimum(r * n_chunks + jnp.minimum(j, n_chunks - 1), total_chunks - 1)

    def col(r, j):
        return jnp.where(r == 0, 0, j)

    row_blk = (chunk_rows, d)
    if len(srcs) == 2:
        in_specs = [pl.BlockSpec(row_blk, lambda r, j: (jnp.minimum(chunk(r, j), src_split - 1), 0)),
                    pl.BlockSpec(row_blk, lambda r, j: (jnp.maximum(chunk(r, j) - src_split, 0), 0))]
    else:
        in_specs = [pl.BlockSpec(row_blk, lambda r, j: (chunk(r, j), 0))]
    args = list(srcs)
    vec = pl.BlockSpec((1, d), lambda r, j: (0, 0))
    if y is not None:
        in_specs += [pl.BlockSpec(row_blk, lambda r, j: (chunk(r, j), 0)), vec]
        args += [y, g_post.reshape(1, d)]
    in_specs.append(vec)
    args.append(g_pre.reshape(1, d))
    w_args, w_specs = [], []
    for w in weights:
        arr, spec, _ = _weight_block(w, d, tn, col)
        w_args.append(arr)
        w_specs.append(spec)
    out_spec = pl.BlockSpec((tm, tn), lambda r, j: (jnp.maximum(r - 1, 0), col(r, j)))
    out_shape = [jax.ShapeDtypeStruct((m, d), jnp.float32), jax.ShapeDtypeStruct((m, n), jnp.bfloat16)]
    out_specs = [pl.BlockSpec(row_blk, lambda r, j: (chunk(r, j), 0)), out_spec]
    blk = ((len(srcs) + (y is not None) + 1) * _nbytes(row_blk, jnp.float32) + _nbytes((d, tn), jnp.bfloat16)
           + _nbytes((tm, tn), jnp.bfloat16) + 2 * _nbytes((tm, tn), jnp.float32) + _nbytes((tm, d), jnp.bfloat16))
    if mode == "in_proj":
        nx = w_args[1].shape[-1]
        w_args[1], w_specs[1], _ = _weight_block(weights[1], d, nx, lambda r, j: 0)
        in_specs += w_specs
        out_shape.append(jax.ShapeDtypeStruct((m, nx), jnp.float32))
        out_specs.append(pl.BlockSpec((tm, nx), lambda r, j: (jnp.maximum(r - 1, 0), 0)))
        blk += _nbytes((d, nx), jnp.bfloat16) + _nbytes((tm, nx), jnp.float32)
    elif mode == "gate_up":
        in_specs += w_specs
        blk += _nbytes((d, tn), jnp.bfloat16) + 2 * _nbytes((tm, tn), jnp.float32)
    else:
        in_specs += w_specs
    args += w_args
    return pl.pallas_call(
        functools.partial(_norm_ahead_kernel, n_src=len(srcs), src_split=src_split, has_y=y is not None,
                          mode=mode, n_row_tiles=n_row_tiles, n_chunks=n_chunks, chunk_rows=chunk_rows),
        out_shape=tuple(out_shape),
        grid=(n_row_tiles + 1, n_col_tiles),
        in_specs=in_specs,
        out_specs=tuple(out_specs),
        scratch_shapes=[pltpu.VMEM((tm, d), jnp.bfloat16), pltpu.VMEM((tm, d), jnp.bfloat16)],
        compiler_params=_params(blk, ("arbitrary", "arbitrary")),
        name="norm_ahead_" + mode,
    )(*args)


def _norm_behind_kernel(*refs, n_pairs, n_row_tiles, n_col_tiles, n_chunks, chunk_rows):
    a_refs = refs[:n_pairs]
    w_refs = refs[n_pairs:2 * n_pairs]
    h_ref, gpost_ref, gpre_ref, h_out_ref, a_out_ref, y_even, y_odd = refs[2 * n_pairs:]
    r = pl.program_id(0)
    j = pl.program_id(1)
    tn = w_refs[0].shape[1]
    d = h_ref.shape[1]

    def multiply(y_dst):
        acc = jnp.dot(a_refs[0][...], w_refs[0][...], preferred_element_type=jnp.float32)
        for p in range(1, n_pairs):
            acc += jnp.dot(a_refs[p][...], w_refs[p][...], preferred_element_type=jnp.float32)
        y_dst[j] = acc

    def norm_chunk(y_src):
        rows = pl.ds(pl.multiple_of(jnp.minimum(j, n_chunks - 1) * chunk_rows, chunk_rows), chunk_rows)
        cols = [slice(c * tn, (c + 1) * tn) for c in range(n_col_tiles)]
        y = [y_src[c, rows, :] for c in range(n_col_tiles)]
        y_scale = lax.rsqrt(sum(jnp.sum(t * t, axis=-1, keepdims=True) for t in y) / d + RMS_EPS)
        x = [h_ref[:, cols[c]] + y[c] * y_scale * gpost_ref[:, cols[c]] for c in range(n_col_tiles)]
        x_scale = lax.rsqrt(sum(jnp.sum(t * t, axis=-1, keepdims=True) for t in x) / d + RMS_EPS)
        for c in range(n_col_tiles):
            h_out_ref[:, cols[c]] = x[c]
            a_out_ref[:, cols[c]] = (x[c] * x_scale * gpre_ref[:, cols[c]]).astype(a_out_ref.dtype)

    bufs = (y_even, y_odd)

    @pl.when(r == 0)
    def _():
        multiply(y_even)

    for parity in range(2):
        @pl.when((r > 0) & (r < n_row_tiles) & (r % 2 == parity))
        def _(parity=parity):
            multiply(bufs[parity])
            norm_chunk(bufs[1 - parity])

    @pl.when(r == n_row_tiles)
    def _():
        norm_chunk(bufs[(n_row_tiles - 1) % 2])


def matmul_norm_behind(a_list, w_list, h, g_post, g_pre, tm, tn, n_chunks):
    m, d = h.shape
    assert m % tm == 0 and d % tn == 0 and tm % n_chunks == 0
    n_row_tiles, n_col_tiles = m // tm, d // tn
    assert n_chunks <= n_col_tiles
    chunk_rows = tm // n_chunks
    total_chunks = n_row_tiles * n_chunks

    def row(r, j):
        return jnp.minimum(r, n_row_tiles - 1)

    def col(r, j):
        return jnp.where(r == n_row_tiles, 0, j)

    def chunk(r, j):
        return jnp.clip((r - 1) * n_chunks + jnp.minimum(j, n_chunks - 1), 0, total_chunks - 1)

    in_specs, blk = [], 0
    for a in a_list:
        in_specs.append(pl.BlockSpec((tm, a.shape[1]), lambda r, j: (row(r, j), 0)))
        blk += _nbytes((tm, a.shape[1]), a.dtype)
    w_args = []
    for a, w in zip(a_list, w_list):
        arr, spec, n = _weight_block(w, a.shape[1], tn, col)
        assert n == d
        w_args.append(arr)
        in_specs.append(spec)
        blk += _nbytes((a.shape[1], tn), arr.dtype)
    chunk_spec = pl.BlockSpec((chunk_rows, d), lambda r, j: (chunk(r, j), 0))
    vec = pl.BlockSpec((1, d), lambda r, j: (0, 0))
    in_specs += [chunk_spec, vec, vec]
    blk += (2 * _nbytes((chunk_rows, d), jnp.float32) + _nbytes((chunk_rows, d), jnp.bfloat16)
            + _nbytes((tm, d), jnp.float32) + _nbytes((tm, tn), jnp.float32))
    return pl.pallas_call(
        functools.partial(_norm_behind_kernel, n_pairs=len(a_list), n_row_tiles=n_row_tiles,
                          n_col_tiles=n_col_tiles, n_chunks=n_chunks, chunk_rows=chunk_rows),
        out_shape=(jax.ShapeDtypeStruct((m, d), jnp.float32), jax.ShapeDtypeStruct((m, d), jnp.bfloat16)),
        grid=(n_row_tiles + 1, n_col_tiles),
        in_specs=in_specs,
        out_specs=(chunk_spec, chunk_spec),
        scratch_shapes=[pltpu.VMEM((n_col_tiles, tm, tn), jnp.float32),
                        pltpu.VMEM((n_col_tiles, tm, tn), jnp.float32)],
        compiler_params=_params(blk, ("arbitrary", "arbitrary")),
        name="matmul_norm_behind",
    )(*a_list, *w_args, h, g_post.reshape(1, d), g_pre.reshape(1, d))


def _matmul_kernel(*refs, n_pairs):
    o_ref = refs[-1]
    acc = jnp.dot(refs[0][...], refs[n_pairs][...], preferred_element_type=jnp.float32)
    for p in range(1, n_pairs):
        acc += jnp.dot(refs[p][...], refs[n_pairs + p][...], preferred_element_type=jnp.float32)
    o_ref[...] = acc.astype(o_ref.dtype)


def matmul(a_list, w_list, out_dtype, tn_pref=1024, tm_pref=TOK_TILE):
    m = a_list[0].shape[0]
    n = w_list[0][0].shape[2] if isinstance(w_list[0], tuple) else w_list[0].shape[1]
    tm = _tile(m, tm_pref)
    tn = _tile(n, tn_pref)
    in_specs, w_args, blk = [], [], 0
    for a in a_list:
        k = a.shape[1]
        in_specs.append(pl.BlockSpec((tm, k), lambda i, j: (i, 0)))
        blk += _nbytes((tm, k), a.dtype)
    for a, w in zip(a_list, w_list):
        arr, spec, _ = _weight_block(w, a.shape[1], tn, lambda i, j: j)
        w_args.append(arr)
        in_specs.append(spec)
        blk += _nbytes((a.shape[1], tn), arr.dtype)
    blk += _nbytes((tm, tn), out_dtype) + _nbytes((tm, tn), jnp.float32)
    return pl.pallas_call(
        functools.partial(_matmul_kernel, n_pairs=len(a_list)),
        out_shape=jax.ShapeDtypeStruct((m, n), out_dtype),
        grid=(m // tm, n // tn),
        in_specs=in_specs,
        out_specs=pl.BlockSpec((tm, tn), lambda i, j: (i, j)),
        compiler_params=_params(blk, ("parallel", "arbitrary")),
        name="matmul",
    )(*a_list, *w_args)


def _log_sigmoid(x):
    return jnp.minimum(x, 0.0) - jnp.log(1.0 + jnp.exp(-jnp.abs(x)))


def _split3_dot(tri, x):
    hi = x.astype(jnp.bfloat16)
    r1 = x - hi.astype(jnp.float32)
    mid = r1.astype(jnp.bfloat16)
    lo = (r1 - mid.astype(jnp.float32)).astype(jnp.bfloat16)
    out = jnp.dot(tri, hi, preferred_element_type=jnp.float32)
    out += jnp.dot(tri, mid, preferred_element_type=jnp.float32)
    out += jnp.dot(tri, lo, preferred_element_type=jnp.float32)
    return out


def _mlstm_kernel(*refs, reverse, n_chunks, seq_first_chunks, dk, dv, combine):
    if combine:
        q_ref, k_ref, v_ref, g_ref, gb_ref, hf_ref, og_ref, gout_ref, o_ref, c_scr, n_scr, m_scr = refs
    else:
        q_ref, k_ref, v_ref, g_ref, gb_ref, o_ref, c_scr, n_scr, m_scr = refs
    L = q_ref.shape[0]
    step = pl.program_id(0)
    chunk = (n_chunks - 1 - step) if reverse else step

    is_start = functools.reduce(jnp.logical_or, [chunk == c for c in seq_first_chunks])

    @pl.when(is_start)
    def _():
        c_scr[...] = jnp.zeros_like(c_scr)
        n_scr[...] = jnp.zeros_like(n_scr)
        m_scr[...] = jnp.zeros_like(m_scr)

    rows = lax.broadcasted_iota(jnp.int32, (L, L), 0)
    cols = lax.broadcasted_iota(jnp.int32, (L, L), 1)
    keep = (cols >= rows) if reverse else (cols <= rows)
    tri = jnp.where(keep, 1.0, 0.0).astype(jnp.bfloat16)

    pre = g_ref[...] + gb_ref[...]
    capped = GATE_CAP * jnp.tanh(pre / GATE_CAP)
    log_f = _log_sigmoid(capped)
    cum = _split3_dot(tri, log_f)
    capped_t = capped.T
    cum_t = cum.T
    edge = 0 if reverse else L - 1
    d = 1 if reverse else 0
    heads = range(M_HEADS)

    q, k, v, c_prev, s, qc = [], [], [], [], [], []
    for h in heads:
        q.append(q_ref[:, h * dk:(h + 1) * dk])
        k.append(k_ref[:, h * dk:(h + 1) * dk] * jnp.asarray(dk ** -0.5, k_ref.dtype))
        v.append(v_ref[:, h * dv:(h + 1) * dv])
        c_prev.append(c_scr[h])
        s.append(lax.dot_general(q[h], k[h], (((1,), (1,)), ((), ())), preferred_element_type=jnp.float32))
        qc.append(jnp.dot(q[h], c_prev[h].astype(jnp.bfloat16), preferred_element_type=jnp.float32))

    p, w_inter, den, m_row = [], [], [], []
    for h in heads:
        il = (2 * d) * M_HEADS + h
        fl = (2 * d + 1) * M_HEADS + h
        b_col = cum[:, fl:fl + 1]
        li_col = capped[:, il:il + 1]
        total = cum[edge:edge + 1, fl:fl + 1]
        src_row = capped_t[il:il + 1, :] - cum_t[fl:fl + 1, :]
        m_prev = m_scr[h, 0:1, 0:1]
        n_prev = n_scr[h, 0:1, :]

        d_log = jnp.where(keep, b_col + src_row, MASK_NEG)
        m_inter = b_col + m_prev
        m_row.append(jnp.maximum(m_inter, jnp.max(d_log, axis=1, keepdims=True)))
        ph = s[h] * jnp.exp(d_log - m_row[h])
        w_inter.append(jnp.exp(m_inter - m_row[h]))
        den.append(w_inter[h] * jnp.sum(q[h].astype(jnp.float32) * n_prev, axis=1, keepdims=True)
                   + jnp.sum(ph, axis=1, keepdims=True))
        p.append(ph.astype(jnp.bfloat16))

        g_col = total - b_col + li_col
        m_new = jnp.maximum(total + m_prev, jnp.max(g_col, axis=0, keepdims=True))
        w_tok = jnp.exp(g_col - m_new)
        decay = jnp.exp(total + m_prev - m_new)
        kw = k[h].astype(jnp.float32) * w_tok
        c_scr[h] = decay * c_prev[h] + lax.dot_general(
            kw.astype(jnp.bfloat16), v[h], (((0,), (0,)), ((), ())), preferred_element_type=jnp.float32)
        n_scr[h] = jnp.broadcast_to(decay * n_prev + jnp.sum(kw, axis=0, keepdims=True), n_scr.shape[1:])
        m_scr[h] = jnp.broadcast_to(m_new, m_scr.shape[1:])

    for h in heads:
        sl = slice(h * dv, (h + 1) * dv)
        num = w_inter[h] * qc[h] + jnp.dot(p[h], v[h], preferred_element_type=jnp.float32)
        x = num / jnp.maximum(jnp.abs(den[h]), jnp.exp(-m_row[h]))
        if combine:
            x = x + hf_ref[:, sl]
            x = x * _rms_scale(x) * gout_ref[:, sl]
            x = x * jax.nn.sigmoid(og_ref[:, sl].astype(jnp.float32))
        o_ref[:, sl] = x.astype(o_ref.dtype)


def mlstm_scan(z, gates, gate_bias, seqs, reverse, dk, dv, combine=None):
    m = z.shape[0]
    L = MLSTM_CHUNK
    qw, vw = M_HEADS * dk, M_HEADS * dv
    n_chunks = m // L
    for s0, t in seqs:
        assert s0 % L == 0 and t % L == 0
    if reverse:
        firsts = tuple((s0 + t) // L - 1 for s0, t in seqs)
    else:
        firsts = tuple(s0 // L for s0, _ in seqs)
    cidx = (lambda i: n_chunks - 1 - i) if reverse else (lambda i: i)
    assert (2 * qw) % vw == 0
    v_blk = (2 * qw) // vw
    in_specs = [pl.BlockSpec((L, qw), lambda i: (cidx(i), 0)),
                pl.BlockSpec((L, qw), lambda i: (cidx(i), 1)),
                pl.BlockSpec((L, vw), lambda i: (cidx(i), v_blk)),
                pl.BlockSpec((L, GATE_LANES), lambda i: (cidx(i), 0)),
                pl.BlockSpec((1, GATE_LANES), lambda i: (0, 0))]
    args = [z, z, z, gates, gate_bias]
    blk = (2 * _nbytes((L, qw), z.dtype) + _nbytes((L, vw), z.dtype) + _nbytes((L, GATE_LANES), jnp.float32)
           + _nbytes((L, vw), jnp.float32) + _nbytes((M_HEADS, dk, dv), jnp.float32))
    out_dtype = jnp.float32
    if combine is not None:
        h_other, g_out = combine
        in_specs += [pl.BlockSpec((L, vw), lambda i: (cidx(i), 0)),
                     pl.BlockSpec((L, vw), lambda i: (cidx(i), v_blk + 1)),
                     pl.BlockSpec((1, vw), lambda i: (0, 0))]
        args += [h_other, z, g_out.reshape(1, vw)]
        blk += _nbytes((L, vw), jnp.float32) + _nbytes((L, vw), z.dtype)
        out_dtype = jnp.bfloat16
    scratch = [pltpu.VMEM((M_HEADS, dk, dv), jnp.float32),
               pltpu.VMEM((M_HEADS, 8, dk), jnp.float32),
               pltpu.VMEM((M_HEADS, 8, LANE), jnp.float32)]
    return pl.pallas_call(
        functools.partial(_mlstm_kernel, reverse=reverse, n_chunks=n_chunks,
                          seq_first_chunks=firsts, dk=dk, dv=dv, combine=combine is not None),
        out_shape=jax.ShapeDtypeStruct((m, vw), out_dtype),
        grid=(n_chunks,),
        in_specs=in_specs,
        out_specs=pl.BlockSpec((L, vw), lambda i: (cidx(i), 0)),
        scratch_shapes=scratch,
        compiler_params=_params(blk, ("arbitrary",)),
        name="mlstm_bwd" if reverse else "mlstm_fwd",
    )(*args)


NA_PAIR = 2
NA_WIN = NA_KR + NA_PAIR - 1
NA_CASES = ((0, 0, 0), (2, 0, 0), (4, 0, 1), (5, 1, 1), (7, 1, 1))
NA_HEAD_GROUP = 4
LOG2E = float(np.log2(np.e))


def _na_kernel(q_ref, k_ref, v_ref, bias_ref, o_ref, *, dh):
    scale2 = dh ** -0.5 * LOG2E
    for h0 in range(0, NA_HEADS, NA_HEAD_GROUP):
        heads = range(h0, h0 + NA_HEAD_GROUP)
        s = {}
        for h in heads:
            sl = slice(h * dh, (h + 1) * dh)
            s[h] = lax.dot_general(q_ref[:, sl], k_ref[:, sl], (((1,), (1,)), ((), ())),
                                   preferred_element_type=jnp.float32)
        e, inv = {}, {}
        for h in heads:
            t = s[h] * scale2 + bias_ref[0, h]
            eh = jnp.exp2(t - jnp.max(t, axis=1, keepdims=True))
            inv[h] = 1.0 / jnp.sum(eh, axis=1, keepdims=True)
            e[h] = eh.astype(jnp.bfloat16)
        for h in heads:
            sl = slice(h * dh, (h + 1) * dh)
            o = jnp.dot(e[h], v_ref[:, sl], preferred_element_type=jnp.float32) * inv[h]
            o_ref[:, sl] = o.astype(o_ref.dtype)


def _na_bias_tables(rpb):
    depth = rpb.shape[0]
    qs = np.arange(GRID_W)
    cs = np.clip(qs - NA_KC // 2, 0, GRID_W - NA_KC)
    ccol = np.arange(GRID_W)
    col_ok = (ccol[None, :] >= cs[:, None]) & (ccol[None, :] < cs[:, None] + NA_KC)
    dc = ccol[None, :] - qs[:, None] + NA_KC - 1
    n_dr, n_dc = 2 * NA_KR - 1, 2 * NA_KC - 1
    col_sel = (dc[None] == np.arange(n_dc)[:, None, None]) & col_ok[None]
    planes = jnp.dot(rpb.astype(jnp.float32).reshape(-1, n_dc),
                     col_sel.reshape(n_dc, GRID_W * GRID_W).astype(jnp.float32),
                     precision=lax.Precision.HIGHEST)
    planes = planes.reshape(depth, NA_HEADS, n_dr, GRID_W, GRID_W)
    planes = jnp.where(col_ok, planes, MASK_NEG) * LOG2E
    planes = jnp.transpose(planes, (0, 1, 3, 2, 4))
    cases = []
    for rel0, *offs in NA_CASES:
        members = []
        for j in range(NA_PAIR):
            dr0 = offs[j] - rel0 - j + NA_KR - 1
            rows = jnp.pad(planes[:, :, :, dr0:dr0 + NA_KR], ((0, 0),) * 3 + ((offs[j], NA_WIN - NA_KR - offs[j]), (0, 0)),
                           constant_values=MASK_NEG * LOG2E)
            members.append(rows.reshape(depth, NA_HEADS, GRID_W, NA_WIN * GRID_W))
        cases.append(jnp.stack(members, axis=2))
    t = jnp.stack(cases, axis=1)
    return t.reshape(depth, len(NA_CASES), NA_HEADS, NA_PAIR * GRID_W, NA_WIN * GRID_W)


def neighborhood_attention(z, bias, seqs, q_col_block, dh):
    m = z.shape[0]
    hw = NA_HEADS * dh
    n_rows = m // GRID_W
    bounds = [(s0 // GRID_W, t // GRID_W) for s0, t in seqs]
    for r0, nr in bounds:
        assert nr % NA_PAIR == 0 and r0 % NA_PAIR == 0 and nr >= NA_KR + 4

    def geometry(g):
        r = g * NA_PAIR
        lo = jnp.int32(0)
        nr_ = jnp.int32(0)
        for r0, nr in bounds:
            inside = (r >= r0) & (r < r0 + nr)
            lo = jnp.where(inside, r0, lo)
            nr_ = jnp.where(inside, nr, nr_)
        top = r - lo
        bot = lo + nr_ - NA_PAIR - r
        u = jnp.clip(r - NA_KR // 2, lo, lo + nr_ - NA_WIN)
        case = jnp.where(top == 0, 0, jnp.where(top == 2, 1, jnp.where(bot == 0, 4, jnp.where(bot == 2, 3, 2))))
        return u, case

    tq, tk = NA_PAIR * GRID_W, NA_WIN * GRID_W

    def kv_spec(col_block):
        return pl.BlockSpec((pl.Element(tk), pl.Element(hw)),
                            lambda g: (geometry(g)[0] * GRID_W, col_block * hw))

    in_specs = [pl.BlockSpec((tq, hw), lambda g: (g, q_col_block)),
                kv_spec(q_col_block + 1), kv_spec(q_col_block + 2),
                pl.BlockSpec((1, NA_HEADS, tq, tk), lambda g: (geometry(g)[1], 0, 0, 0))]
    blk = (2 * _nbytes((tq, hw), z.dtype) + 2 * _nbytes((tk, hw), z.dtype)
           + 2 * _nbytes((NA_HEADS, tq, tk), jnp.float32))
    return pl.pallas_call(
        functools.partial(_na_kernel, dh=dh),
        out_shape=jax.ShapeDtypeStruct((m, hw), jnp.bfloat16),
        grid=(n_rows // NA_PAIR,),
        in_specs=in_specs,
        out_specs=pl.BlockSpec((tq, hw), lambda g: (g, 0)),
        compiler_params=_params(blk, ("parallel",)),
        name="neighborhood_attention",
    )(z, z, z, bias)


def _xattn_kernel(q_ref, k_ref, v_ref, o_ref, *, dh):
    scale = dh ** -0.5
    for h in range(X_HEADS):
        sl = slice(h * dh, (h + 1) * dh)
        s = lax.dot_general(q_ref[:, sl], k_ref[:, sl], (((1,), (1,)), ((), ())),
                            preferred_element_type=jnp.float32) * scale
        s = s - jnp.max(s, axis=1, keepdims=True)
        e = jnp.exp(s)
        p = e / jnp.sum(e, axis=1, keepdims=True)
        o_ref[:, sl] = jnp.dot(p.astype(v_ref.dtype), v_ref[:, sl],
                               preferred_element_type=jnp.float32).astype(o_ref.dtype)


def memory_attention(q, k, v, seqs, n_mem):
    m, xw = q.shape
    dh = xw // X_HEADS
    tm = XATTN_TILE
    for s0, t in seqs:
        tm = min(tm, _tile(t, tm))
    starts = [s0 // tm for s0, _ in seqs]
    for s0, _ in seqs:
        assert s0 % tm == 0

    def seq_of(i):
        s = jnp.int32(0)
        for idx, st in enumerate(starts):
            s = jnp.where(i >= st, idx, s)
        return s

    blk = 2 * _nbytes((tm, xw), q.dtype) + 2 * _nbytes((n_mem, xw), k.dtype) + 2 * _nbytes((tm, n_mem), jnp.float32)
    return pl.pallas_call(
        functools.partial(_xattn_kernel, dh=dh),
        out_shape=jax.ShapeDtypeStruct((m, xw), jnp.bfloat16),
        grid=(m // tm,),
        in_specs=[pl.BlockSpec((tm, xw), lambda i: (i, 0)),
                  pl.BlockSpec((n_mem, xw), lambda i: (seq_of(i), 0)),
                  pl.BlockSpec((n_mem, xw), lambda i: (seq_of(i), 0))],
        out_specs=pl.BlockSpec((tm, xw), lambda i: (i, 0)),
        compiler_params=_params(blk, ("parallel",)),
        name="memory_attention",
    )(q, k, v)


def kernel(x_prompt, x_sample, mem_prompt, mem_sample, w_in, w_out, g_pre_mix, g_post_mix, i_bias, f_bias, g_mlstm_out, rpb, g_pre_xattn, g_post_xattn, g_mem, wq_x, wk_x, wv_x, wo_x, g_pre_ffn, g_post_ffn, w_gate, w_up, w_down):
    bf16 = jnp.bfloat16
    depth = w_in.shape[0]
    d = x_prompt.shape[-1]
    bp, tp, _ = x_prompt.shape
    bs, ts, _ = x_sample.shape
    n_mem = mem_prompt.shape[1]
    mix_w = w_out.shape[1]
    v_w = mix_w // 2
    dv = v_w // M_HEADS
    dk = dv // 2
    qk_w = M_HEADS * dk
    na_dh = v_w // NA_HEADS
    n_gates = 4 * M_HEADS
    d_ff = w_gate.shape[2]
    assert w_in.shape[2] == 2 * qk_w + 2 * v_w + n_gates + 3 * v_w

    seqs = [(b * tp, tp) for b in range(bp)] + [(bp * tp + b * ts, ts) for b in range(bs)]
    mem =jnp.concatenate([mem_prompt.reshape(bp * n_mem, d), mem_sample.reshape(bs * n_mem, d)], axis=0)

    m_cols = 2 * qk_w + 2 * v_w
    w_main = jnp.concatenate([w_in[:, :, :m_cols], w_in[:, :, m_cols + n_gates:]], axis=2).astype(bf16)
    w_gates = jnp.pad(w_in[:, :, m_cols:m_cols + n_gates], ((0, 0), (0, 0), (0, GATE_LANES - n_gates))).astype(bf16)
    gate_bias = jnp.concatenate([i_bias[:, 0], f_bias[:, 0], i_bias[:, 1], f_bias[:, 1]], axis=1)
    gate_bias = jnp.pad(gate_bias, ((0, 0), (0, GATE_LANES - n_gates))).astype(jnp.float32)
    w_out_b = w_out.astype(bf16)
    wq, wk, wv, wo = (w.astype(bf16) for w in (wq_x, wk_x, wv_x, wo_x))
    wg, wu, wd = (w.astype(bf16) for w in (w_gate, w_up, w_down))
    na_bias = _na_bias_tables(rpb)

    tm = _tile(np.gcd(bp * tp, bs * ts), TOK_TILE)
    srcs, y, g_post = [x_prompt.reshape(bp * tp, d), x_sample.reshape(bs * ts, d)], None, None
    for l in range(depth):
        tm_in = _tile(tm, IN_PROJ_TM)
        h, z, gates = norm_ahead_proj(srcs, y, g_post, g_pre_mix[l], ((w_main, l, 0), (w_gates, l, 0)), "in_proj",
                                      tm_in, _col_tile(w_main.shape[2], IN_PROJ_TN), _tile(tm_in, IN_PROJ_CHUNKS))
        gb = gate_bias[l].reshape(1, GATE_LANES)
        h_fw = mlstm_scan(z, gates, gb, seqs, False, dk, dv)
        y_m = mlstm_scan(z, gates, gb, seqs, True, dk, dv, combine=(h_fw, g_mlstm_out[l]))
        y_n = neighborhood_attention(z, na_bias[l], seqs, m_cols // v_w, na_dh)
        tm_out = _tile(tm, OUT_PROJ_TM)
        tn_out = _col_tile(d, OUT_PROJ_TN)
        h, a = matmul_norm_behind([y_m, y_n], [(w_out_b, l, 0), (w_out_b, l, 1)], h, g_post_mix[l], g_pre_xattn[l],
                                  tm_out, tn_out, min(d // tn_out, OUT_PROJ_CHUNKS))
        q = matmul([a], [(wq, l, 0)], bf16)
        mem_n = rmsnorm_cast(mem, g_mem[l])
        k = matmul([mem_n], [(wk, l, 0)], bf16)
        v = matmul([mem_n], [(wv, l, 0)], bf16)
        o = memory_attention(q, k, v, seqs, n_mem)
        y = matmul([o], [(wo, l, 0)], jnp.float32)
        h, hidden = norm_ahead_proj([h], y, g_post_xattn[l], g_pre_ffn[l], ((wg, l, 0), (wu, l, 0)), "gate_up",
                                    tm, _tile(d_ff, FF_UP_TN), _tile(tm, FF_UP_CHUNKS))
        y = matmul([hidden], [(wd, l, 0)], jnp.float32, tn_pref=FF_DOWN_TILE, tm_pref=FF_DOWN_TILE)
        srcs, g_post = [h], g_post_ffn[l]

    out_p, out_s = residual_split(h, y, g_post, bp * tp)
    return (out_p.reshape(bp, tp, d), out_s.reshape(bs, ts, d))
```

```python
import functools

import jax
import jax.numpy as jnp
import numpy as np
from jax import lax
from jax.experimental import pallas as pl
from jax.experimental.pallas import tpu as pltpu

GRID_W = 64
M_HEADS = 4
GATE_CAP = 15.0
NA_HEADS = 16
NA_KR = 8
NA_KC = 16
X_HEADS = 4
RMS_EPS = 1e-6
GATE_LANES = 128
MASK_NEG = -1e30

V7X_VMEM_BYTES = 64 * 1024 * 1024
V7X_VMEM_BUDGET = 56 * 1024 * 1024
LANE = 128

MLSTM_CHUNK = 256
TOK_TILE = 1024
ELEM_TILE = 256
XATTN_TILE = 512
OUT_PROJ_TM = 512
OUT_PROJ_TN = 1024
OUT_PROJ_CHUNKS = 4
IN_PROJ_TM = 512
IN_PROJ_TN = 1536
IN_PROJ_CHUNKS = 8
FF_UP_TN = 256
FF_UP_CHUNKS = 32
FF_ROW_GROUPS = 4
FF_DOWN_TILE = 512


def _params(block_bytes, semantics):
    limit = min(V7X_VMEM_BUDGET, int(2 * block_bytes) + 16 * 1024 * 1024)
    return pltpu.CompilerParams(dimension_semantics=semantics, vmem_limit_bytes=limit)


def _nbytes(shape, dtype):
    return int(np.prod(shape)) * jnp.dtype(dtype).itemsize


def _tile(m, pref):
    t = min(m, pref)
    while m % t:
        t //= 2
    return t


def _col_tile(n, pref):
    t = (min(n, pref) // LANE) * LANE
    while n % t:
        t -= LANE
    return t


def _weight_block(w, k, tn, col_index):
    if isinstance(w, tuple):
        arr, layer, kb = w
        return arr, pl.BlockSpec((None, k, tn), lambda *g: (layer, kb, col_index(*g))), arr.shape[2]
    return w, pl.BlockSpec((k, tn), lambda *g: (0, col_index(*g))), w.shape[1]


def _rms_scale(x):
    return lax.rsqrt(jnp.mean(x * x, axis=-1, keepdims=True) + RMS_EPS)


def _rmsnorm_cast_kernel(x_ref, g_ref, o_ref):
    x = x_ref[...]
    o_ref[...] = (x * _rms_scale(x) * g_ref[...]).astype(o_ref.dtype)


def rmsnorm_cast(x, g):
    m, d = x.shape
    tm = _tile(m, ELEM_TILE)
    blk = _nbytes((tm, d), jnp.float32) + _nbytes((tm, d), jnp.bfloat16)
    return pl.pallas_call(
        _rmsnorm_cast_kernel,
        out_shape=jax.ShapeDtypeStruct((m, d), jnp.bfloat16),
        grid=(m // tm,),
        in_specs=[pl.BlockSpec((tm, d), lambda i: (i, 0)),
                  pl.BlockSpec((1, d), lambda i: (0, 0))],
        out_specs=pl.BlockSpec((tm, d), lambda i: (i, 0)),
        compiler_params=_params(blk, ("parallel",)),
        name="rmsnorm_cast",
    )(x, g.reshape(1, d))


def _residual_split_kernel(h_ref, y_ref, gpost_ref, a_out_ref, b_out_ref, *, n_first):
    y = y_ref[...]
    x = h_ref[...] + y * _rms_scale(y) * gpost_ref[...]
    i = pl.program_id(0)

    @pl.when(i < n_first)
    def _():
        a_out_ref[...] = x

    @pl.when(i >= n_first)
    def _():
        b_out_ref[...] = x


def residual_split(h, y, g_post, rows_first):
    m, d = h.shape
    tm = _tile(np.gcd(rows_first, m - rows_first), ELEM_TILE)
    n_first = rows_first // tm
    row = pl.BlockSpec((tm, d), lambda i: (i, 0))
    vec = pl.BlockSpec((1, d), lambda i: (0, 0))
    blk = 4 * _nbytes((tm, d), jnp.float32)
    return pl.pallas_call(
        functools.partial(_residual_split_kernel, n_first=n_first),
        out_shape=(jax.ShapeDtypeStruct((rows_first, d), jnp.float32),
                   jax.ShapeDtypeStruct((m - rows_first, d), jnp.float32)),
        grid=(m // tm,), in_specs=[row, row, vec],
        out_specs=(pl.BlockSpec((tm, d), lambda i: (jnp.minimum(i, n_first - 1), 0)),
                   pl.BlockSpec((tm, d), lambda i: (jnp.maximum(i - n_first, 0), 0))),
        compiler_params=_params(blk, ("arbitrary",)),
        name="residual_split",
    )(h, y, g_post.reshape(1, d))


def _norm_ahead_kernel(*refs, n_src, src_split, has_y, mode, n_row_tiles, n_chunks, chunk_rows):
    refs = list(refs)
    src_refs = [refs.pop(0) for _ in range(n_src)]
    y_ref, gpost_ref = (refs.pop(0), refs.pop(0)) if has_y else (None, None)
    gpre_ref = refs.pop(0)
    if mode == "in_proj":
        w_ref, wx_ref, h_out_ref, o_ref, ox_ref, a_even, a_odd = refs
    elif mode == "gate_up":
        w_ref, wu_ref, h_out_ref, o_ref, a_even, a_odd = refs
    else:
        w_ref, h_out_ref, o_ref, a_even, a_odd = refs
    r = pl.program_id(0)
    j = pl.program_id(1)

    def norm_chunk(a_dst):
        cj = jnp.minimum(j, n_chunks - 1)
        if n_src == 2:
            x = jnp.where(r * n_chunks + cj < src_split, src_refs[0][...], src_refs[1][...])
        else:
            x = src_refs[0][...]
        if has_y:
            y = y_ref[...]
            x = x + y * _rms_scale(y) * gpost_ref[...]
        h_out_ref[...] = x
        a = (x * _rms_scale(x) * gpre_ref[...]).astype(a_dst.dtype)
        a_dst[pl.ds(pl.multiple_of(cj * chunk_rows, chunk_rows), chunk_rows), :] = a

    def project(a_src):
        a = a_src[...]
        groups = FF_ROW_GROUPS if mode == "gate_up" else 1
        step = a.shape[0] // groups
        for rows in (slice(i * step, (i + 1) * step) for i in range(groups)):
            acc = jnp.dot(a[rows], w_ref[...], preferred_element_type=jnp.float32)
            if mode == "gate_up":
                acc = acc * jax.nn.sigmoid(acc) * jnp.dot(a[rows], wu_ref[...], preferred_element_type=jnp.float32)
            o_ref[rows, :] = acc.astype(o_ref.dtype)

    @pl.when(r == 0)
    def _():
        norm_chunk(a_even)

    for parity, (a_dst, a_src) in enumerate(((a_even, a_odd), (a_odd, a_even))):
        @pl.when((r > 0) & (r % 2 == parity))
        def _(a_dst=a_dst, a_src=a_src):
            norm_chunk(a_dst)
            project(a_src)

        if mode == "in_proj":
            @pl.when((r > 0) & (r % 2 == parity) & (j == 0))
            def _(a_src=a_src):
                ox_ref[...] = jnp.dot(a_src[...], wx_ref[...], preferred_element_type=jnp.float32)


def norm_ahead_proj(srcs, y, g_post, g_pre, weights, mode, tm, tn, n_chunks):
    d = srcs[0].shape[1]
    m = sum(s.shape[0] for s in srcs)
    n = weights[0][0].shape[2] if isinstance(weights[0], tuple) else weights[0].shape[1]
    assert m % tm == 0 and n % tn == 0 and tm % n_chunks == 0
    n_row_tiles, n_col_tiles = m // tm, n // tn
    assert n_chunks <= n_col_tiles
    chunk_rows = tm // n_chunks
    total_chunks = n_row_tiles * n_chunks
    src_split = srcs[0].shape[0] // chunk_rows
    for s in srcs:
        assert s.shape[0] % chunk_rows == 0

    def chunk(r, j):
        return jnp.minimum(r * n_chunks + jnp.minimum(j, n_chunks - 1), total_chunks - 1)

    def col(r, j):
        return jnp.where(r == 0, 0, j)

    row_blk = (chunk_rows, d)
    if len(srcs) == 2:
        in_specs = [pl.BlockSpec(row_blk, lambda r, j: (jnp.minimum(chunk(r, j), src_split - 1), 0)),
                    pl.BlockSpec(row_blk, lambda r, j: (jnp.maximum(chunk(r, j) - src_split, 0), 0))]
    else:
        in_specs = [pl.BlockSpec(row_blk, lambda r, j: (chunk(r, j), 0))]
    args = list(srcs)
    vec = pl.BlockSpec((1, d), lambda r, j: (0, 0))
    if y is not None:
        in_specs += [pl.BlockSpec(row_blk, lambda r, j: (chunk(r, j), 0)), vec]
        args += [y, g_post.reshape(1, d)]
    in_specs.append(vec)
    args.append(g_pre.reshape(1, d))
    w_args, w_specs = [], []
    for w in weights:
        arr, spec, _ = _weight_block(w, d, tn, col)
        w_args.append(arr)
        w_specs.append(spec)
    out_spec = pl.BlockSpec((tm, tn), lambda r, j: (jnp.maximum(r - 1, 0), col(r, j)))
    out_shape = [jax.ShapeDtypeStruct((m, d), jnp.float32), jax.ShapeDtypeStruct((m, n), jnp.bfloat16)]
    out_specs = [pl.BlockSpec(row_blk, lambda r, j: (chunk(r, j), 0)), out_spec]
    blk = ((len(srcs) + (y is not None) + 1) * _nbytes(row_blk, jnp.float32) + _nbytes((d, tn), jnp.bfloat16)
           + _nbytes((tm, tn), jnp.bfloat16) + 2 * _nbytes((tm, tn), jnp.float32) + _nbytes((tm, d), jnp.bfloat16))
    if mode == "in_proj":
        nx = w_args[1].shape[-1]
        w_args[1], w_specs[1], _ = _weight_block(weights[1], d, nx, lambda r, j: 0)
        in_specs += w_specs
        out_shape.append(jax.ShapeDtypeStruct((m, nx), jnp.float32))
        out_specs.append(pl.BlockSpec((tm, nx), lambda r, j: (jnp.maximum(r - 1, 0), 0)))
        blk += _nbytes((d, nx), jnp.bfloat16) + _nbytes((tm, nx), jnp.float32)
    elif mode == "gate_up":
        in_specs += w_specs
        blk += _nbytes((d, tn), jnp.bfloat16) + 2 * _nbytes((tm, tn), jnp.float32)
    else:
        in_specs += w_specs
    args += w_args
    return pl.pallas_call(
        functools.partial(_norm_ahead_kernel, n_src=len(srcs), src_split=src_split, has_y=y is not None,
                          mode=mode, n_row_tiles=n_row_tiles, n_chunks=n_chunks, chunk_rows=chunk_rows),
        out_shape=tuple(out_shape),
        grid=(n_row_tiles + 1, n_col_tiles),
        in_specs=in_specs,
        out_specs=tuple(out_specs),
        scratch_shapes=[pltpu.VMEM((tm, d), jnp.bfloat16), pltpu.VMEM((tm, d), jnp.bfloat16)],
        compiler_params=_params(blk, ("arbitrary", "arbitrary")),
        name="norm_ahead_" + mode,
    )(*args)


def _norm_behind_kernel(*refs, n_pairs, n_row_tiles, n_col_tiles, n_chunks, chunk_rows):
    a_refs = refs[:n_pairs]
    w_refs = refs[n_pairs:2 * n_pairs]
    h_ref, gpost_ref, gpre_ref, h_out_ref, a_out_ref, y_even, y_odd = refs[2 * n_pairs:]
    r = pl.program_id(0)
    j = pl.program_id(1)
    tn = w_refs[0].shape[1]
    d = h_ref.shape[1]

    def multiply(y_dst):
        acc = jnp.dot(a_refs[0][...], w_refs[0][...], preferred_element_type=jnp.float32)
        for p in range(1, n_pairs):
            acc += jnp.dot(a_refs[p][...], w_refs[p][...], preferred_element_type=jnp.float32)
        y_dst[j] = acc

    def norm_chunk(y_src):
        rows = pl.ds(pl.multiple_of(jnp.minimum(j, n_chunks - 1) * chunk_rows, chunk_rows), chunk_rows)
        cols = [slice(c * tn, (c + 1) * tn) for c in range(n_col_tiles)]
        y = [y_src[c, rows, :] for c in range(n_col_tiles)]
        y_scale = lax.rsqrt(sum(jnp.sum(t * t, axis=-1, keepdims=True) for t in y) / d + RMS_EPS)
        x = [h_ref[:, cols[c]] + y[c] * y_scale * gpost_ref[:, cols[c]] for c in range(n_col_tiles)]
        x_scale = lax.rsqrt(sum(jnp.sum(t * t, axis=-1, keepdims=True) for t in x) / d + RMS_EPS)
        for c in range(n_col_tiles):
            h_out_ref[:, cols[c]] = x[c]
            a_out_ref[:, cols[c]] = (x[c] * x_scale * gpre_ref[:, cols[c]]).astype(a_out_ref.dtype)

    bufs = (y_even, y_odd)

    @pl.when(r == 0)
    def _():
        multiply(y_even)

    for parity in range(2):
        @pl.when((r > 0) & (r < n_row_tiles) & (r % 2 == parity))
        def _(parity=parity):
            multiply(bufs[parity])
            norm_chunk(bufs[1 - parity])

    @pl.when(r == n_row_tiles)
    def _():
        norm_chunk(bufs[(n_row_tiles - 1) % 2])


def matmul_norm_behind(a_list, w_list, h, g_post, g_pre, tm, tn, n_chunks):
    m, d = h.shape
    assert m % tm == 0 and d % tn == 0 and tm % n_chunks == 0
    n_row_tiles, n_col_tiles = m // tm, d // tn
    assert n_chunks <= n_col_tiles
    chunk_rows = tm // n_chunks
    total_chunks = n_row_tiles * n_chunks

    def row(r, j):
        return jnp.minimum(r, n_row_tiles - 1)

    def col(r, j):
        return jnp.where(r == n_row_tiles, 0, j)

    def chunk(r, j):
        return jnp.clip((r - 1) * n_chunks + jnp.minimum(j, n_chunks - 1), 0, total_chunks - 1)

    in_specs, blk = [], 0
    for a in a_list:
        in_specs.append(pl.BlockSpec((tm, a.shape[1]), lambda r, j: (row(r, j), 0)))
        blk += _nbytes((tm, a.shape[1]), a.dtype)
    w_args = []
    for a, w in zip(a_list, w_list):
        arr, spec, n = _weight_block(w, a.shape[1], tn, col)
        assert n == d
        w_args.append(arr)
        in_specs.append(spec)
        blk += _nbytes((a.shape[1], tn), arr.dtype)
    chunk_spec = pl.BlockSpec((chunk_rows, d), lambda r, j: (chunk(r, j), 0))
    vec = pl.BlockSpec((1, d), lambda r, j: (0, 0))
    in_specs += [chunk_spec, vec, vec]
    blk += (2 * _nbytes((chunk_rows, d), jnp.float32) + _nbytes((chunk_rows, d), jnp.bfloat16)
            + _nbytes((tm, d), jnp.float32) + _nbytes((tm, tn), jnp.float32))
    return pl.pallas_call(
        functools.partial(_norm_behind_kernel, n_pairs=len(a_list), n_row_tiles=n_row_tiles,
                          n_col_tiles=n_col_tiles, n_chunks=n_chunks, chunk_rows=chunk_rows),
        out_shape=(jax.ShapeDtypeStruct((m, d), jnp.float32), jax.ShapeDtypeStruct((m, d), jnp.bfloat16)),
        grid=(n_row_tiles + 1, n_col_tiles),
        in_specs=in_specs,
        out_specs=(chunk_spec, chunk_spec),
        scratch_shapes=[pltpu.VMEM((n_col_tiles, tm, tn), jnp.float32),
                        pltpu.VMEM((n_col_tiles, tm, tn), jnp.float32)],
        compiler_params=_params(blk, ("arbitrary", "arbitrary")),
        name="matmul_norm_behind",
    )(*a_list, *w_args, h, g_post.reshape(1, d), g_pre.reshape(1, d))


def _matmul_kernel(*refs, n_pairs):
    o_ref = refs[-1]
    acc = jnp.dot(refs[0][...], refs[n_pairs][...], preferred_element_type=jnp.float32)
    for p in range(1, n_pairs):
        acc += jnp.dot(refs[p][...], refs[n_pairs + p][...], preferred_element_type=jnp.float32)
    o_ref[...] = acc.astype(o_ref.dtype)


def matmul(a_list, w_list, out_dtype, tn_pref=1024, tm_pref=TOK_TILE):
    m = a_list[0].shape[0]
    n = w_list[0][0].shape[2] if isinstance(w_list[0], tuple) else w_list[0].shape[1]
    tm = _tile(m, tm_pref)
    tn = _tile(n, tn_pref)
    in_specs, w_args, blk = [], [], 0
    for a in a_list:
        k = a.shape[1]
        in_specs.append(pl.BlockSpec((tm, k), lambda i, j: (i, 0)))
        blk += _nbytes((tm, k), a.dtype)
    for a, w in zip(a_list, w_list):
        arr, spec, _ = _weight_block(w, a.shape[1], tn, lambda i, j: j)
        w_args.append(arr)
        in_specs.append(spec)
        blk += _nbytes((a.shape[1], tn), arr.dtype)
    blk += _nbytes((tm, tn), out_dtype) + _nbytes((tm, tn), jnp.float32)
    return pl.pallas_call(
        functools.partial(_matmul_kernel, n_pairs=len(a_list)),
        out_shape=jax.ShapeDtypeStruct((m, n), out_dtype),
        grid=(m // tm, n // tn),
        in_specs=in_specs,
        out_specs=pl.BlockSpec((tm, tn), lambda i, j: (i, j)),
        compiler_params=_params(blk, ("parallel", "arbitrary")),
        name="matmul",
    )(*a_list, *w_args)


def _log_sigmoid(x):
    return jnp.minimum(x, 0.0) - jnp.log(1.0 + jnp.exp(-jnp.abs(x)))


def _split3_dot(tri, x):
    hi = x.astype(jnp.bfloat16)
    r1 = x - hi.astype(jnp.float32)
    mid = r1.astype(jnp.bfloat16)
    lo = (r1 - mid.astype(jnp.float32)).astype(jnp.bfloat16)
    out = jnp.dot(tri, hi, preferred_element_type=jnp.float32)
    out += jnp.dot(tri, mid, preferred_element_type=jnp.float32)
    out += jnp.dot(tri, lo, preferred_element_type=jnp.float32)
    return out


def _mlstm_kernel(*refs, reverse, n_chunks, seq_first_chunks, dk, dv, combine):
    if combine:
        q_ref, k_ref, v_ref, g_ref, gb_ref, hf_ref, og_ref, gout_ref, o_ref, c_scr, n_scr, m_scr = refs
    else:
        q_ref, k_ref, v_ref, g_ref, gb_ref, o_ref, c_scr, n_scr, m_scr = refs
    L = q_ref.shape[0]
    step = pl.program_id(0)
    chunk = (n_chunks - 1 - step) if reverse else step

    is_start = functools.reduce(jnp.logical_or, [chunk == c for c in seq_first_chunks])

    @pl.when(is_start)
    def _():
        c_scr[...] = jnp.zeros_like(c_scr)
        n_scr[...] = jnp.zeros_like(n_scr)
        m_scr[...] = jnp.zeros_like(m_scr)

    rows = lax.broadcasted_iota(jnp.int32, (L, L), 0)
    cols = lax.broadcasted_iota(jnp.int32, (L, L), 1)
    keep = (cols >= rows) if reverse else (cols <= rows)
    tri = jnp.where(keep, 1.0, 0.0).astype(jnp.bfloat16)

    pre = g_ref[...] + gb_ref[...]
    capped = GATE_CAP * jnp.tanh(pre / GATE_CAP)
    log_f = _log_sigmoid(capped)
    cum = _split3_dot(tri, log_f)
    capped_t = capped.T
    cum_t = cum.T
    edge = 0 if reverse else L - 1
    d = 1 if reverse else 0
    heads = range(M_HEADS)

    q, k, v, c_prev, s, qc = [], [], [], [], [], []
    for h in heads:
        q.append(q_ref[:, h * dk:(h + 1) * dk])
        k.append(k_ref[:, h * dk:(h + 1) * dk] * jnp.asarray(dk ** -0.5, k_ref.dtype))
        v.append(v_ref[:, h * dv:(h + 1) * dv])
        c_prev.append(c_scr[h])
        s.append(lax.dot_general(q[h], k[h], (((1,), (1,)), ((), ())), preferred_element_type=jnp.float32))
        qc.append(jnp.dot(q[h], c_prev[h].astype(jnp.bfloat16), preferred_element_type=jnp.float32))

    p, w_inter, den, m_row = [], [], [], []
    for h in heads:
        il = (2 * d) * M_HEADS + h
        fl = (2 * d + 1) * M_HEADS + h
        b_col = cum[:, fl:fl + 1]
        li_col = capped[:, il:il + 1]
        total = cum[edge:edge + 1, fl:fl + 1]
        src_row = capped_t[il:il + 1, :] - cum_t[fl:fl + 1, :]
        m_prev = m_scr[h, 0:1, 0:1]
        n_prev = n_scr[h, 0:1, :]

        d_log = jnp.where(keep, b_col + src_row, MASK_NEG)
        m_inter = b_col + m_prev
        m_row.append(jnp.maximum(m_inter, jnp.max(d_log, axis=1, keepdims=True)))
        ph = s[h] * jnp.exp(d_log - m_row[h])
        w_inter.append(jnp.exp(m_inter - m_row[h]))
        den.append(w_inter[h] * jnp.sum(q[h].astype(jnp.float32) * n_prev, axis=1, keepdims=True)
                   + jnp.sum(ph, axis=1, keepdims=True))
        p.append(ph.astype(jnp.bfloat16))

        g_col = total - b_col + li_col
        m_new = jnp.maximum(total + m_prev, jnp.max(g_col, axis=0, keepdims=True))
        w_tok = jnp.exp(g_col - m_new)
        decay = jnp.exp(total + m_prev - m_new)
        kw = k[h].astype(jnp.float32) * w_tok
        c_scr[h] = decay * c_prev[h] + lax.dot_general(
            kw.astype(jnp.bfloat16), v[h], (((0,), (0,)), ((), ())), preferred_element_type=jnp.float32)
        n_scr[h] = jnp.broadcast_to(decay * n_prev + jnp.sum(kw, axis=0, keepdims=True), n_scr.shape[1:])
        m_scr[h] = jnp.broadcast_to(m_new, m_scr.shape[1:])

    for h in heads:
        sl = slice(h * dv, (h + 1) * dv)
        num = w_inter[h] * qc[h] + jnp.dot(p[h], v[h], preferred_element_type=jnp.float32)
        x = num / jnp.maximum(jnp.abs(den[h]), jnp.exp(-m_row[h]))
        if combine:
            x = x + hf_ref[:, sl]
            x = x * _rms_scale(x) * gout_ref[:, sl]
            x = x * jax.nn.sigmoid(og_ref[:, sl].astype(jnp.float32))
        o_ref[:, sl] = x.astype(o_ref.dtype)


def mlstm_scan(z, gates, gate_bias, seqs, reverse, dk, dv, combine=None):
    m = z.shape[0]
    L = MLSTM_CHUNK
    qw, vw = M_HEADS * dk, M_HEADS * dv
    n_chunks = m // L
    for s0, t in seqs:
        assert s0 % L == 0 and t % L == 0
    if reverse:
        firsts = tuple((s0 + t) // L - 1 for s0, t in seqs)
    else:
        firsts = tuple(s0 // L for s0, _ in seqs)
    cidx = (lambda i: n_chunks - 1 - i) if reverse else (lambda i: i)
    assert (2 * qw) % vw == 0
    v_blk = (2 * qw) // vw
    in_specs = [pl.BlockSpec((L, qw), lambda i: (cidx(i), 0)),
                pl.BlockSpec((L, qw), lambda i: (cidx(i), 1)),
                pl.BlockSpec((L, vw), lambda i: (cidx(i), v_blk)),
                pl.BlockSpec((L, GATE_LANES), lambda i: (cidx(i), 0)),
                pl.BlockSpec((1, GATE_LANES), lambda i: (0, 0))]
    args = [z, z, z, gates, gate_bias]
    blk = (2 * _nbytes((L, qw), z.dtype) + _nbytes((L, vw), z.dtype) + _nbytes((L, GATE_LANES), jnp.float32)
           + _nbytes((L, vw), jnp.float32) + _nbytes((M_HEADS, dk, dv), jnp.float32))
    out_dtype = jnp.float32
    if combine is not None:
        h_other, g_out = combine
        in_specs += [pl.BlockSpec((L, vw), lambda i: (cidx(i), 0)),
                     pl.BlockSpec((L, vw), lambda i: (cidx(i), v_blk + 1)),
                     pl.BlockSpec((1, vw), lambda i: (0, 0))]
        args += [h_other, z, g_out.reshape(1, vw)]
        blk += _nbytes((L, vw), jnp.float32) + _nbytes((L, vw), z.dtype)
        out_dtype = jnp.bfloat16
    scratch = [pltpu.VMEM((M_HEADS, dk, dv), jnp.float32),
               pltpu.VMEM((M_HEADS, 8, dk), jnp.float32),
               pltpu.VMEM((M_HEADS, 8, LANE), jnp.float32)]
    return pl.pallas_call(
        functools.partial(_mlstm_kernel, reverse=reverse, n_chunks=n_chunks,
                          seq_first_chunks=firsts, dk=dk, dv=dv, combine=combine is not None),
        out_shape=jax.ShapeDtypeStruct((m, vw), out_dtype),
        grid=(n_chunks,),
        in_specs=in_specs,
        out_specs=pl.BlockSpec((L, vw), lambda i: (cidx(i), 0)),
        scratch_shapes=scratch,
        compiler_params=_params(blk, ("arbitrary",)),
        name="mlstm_bwd" if reverse else "mlstm_fwd",
    )(*args)


NA_PAIR = 2
NA_WIN = NA_KR + NA_PAIR - 1
NA_CASES = ((0, 0, 0), (2, 0, 0), (4, 0, 1), (5, 1, 1), (7, 1, 1))
NA_HEAD_GROUP = 4
LOG2E = float(np.log2(np.e))


def _na_kernel(q_ref, k_ref, v_ref, bias_ref, o_ref, *, dh):
    scale2 = dh ** -0.5 * LOG2E
    for h0 in range(0, NA_HEADS, NA_HEAD_GROUP):
        heads = range(h0, h0 + NA_HEAD_GROUP)
        s = {}
        for h in heads:
            sl = slice(h * dh, (h + 1) * dh)
            s[h] = lax.dot_general(q_ref[:, sl], k_ref[:, sl], (((1,), (1,)), ((), ())),
                                   preferred_element_type=jnp.float32)
        e, inv = {}, {}
        for h in heads:
            t = s[h] * scale2 + bias_ref[0, h]
            eh = jnp.exp2(t - jnp.max(t, axis=1, keepdims=True))
            inv[h] = 1.0 / jnp.sum(eh, axis=1, keepdims=True)
            e[h] = eh.astype(jnp.bfloat16)
        for h in heads:
            sl = slice(h * dh, (h + 1) * dh)
            o = jnp.dot(e[h], v_ref[:, sl], preferred_element_type=jnp.float32) * inv[h]
            o_ref[:, sl] = o.astype(o_ref.dtype)


def _na_bias_tables(rpb):
    depth = rpb.shape[0]
    qs = np.arange(GRID_W)
    cs = np.clip(qs - NA_KC // 2, 0, GRID_W - NA_KC)
    ccol = np.arange(GRID_W)
    col_ok = (ccol[None, :] >= cs[:, None]) & (ccol[None, :] < cs[:, None] + NA_KC)
    dc = ccol[None, :] - qs[:, None] + NA_KC - 1
    n_dr, n_dc = 2 * NA_KR - 1, 2 * NA_KC - 1
    col_sel = (dc[None] == np.arange(n_dc)[:, None, None]) & col_ok[None]
    planes = jnp.dot(rpb.astype(jnp.float32).reshape(-1, n_dc),
                     col_sel.reshape(n_dc, GRID_W * GRID_W).astype(jnp.float32),
                     precision=lax.Precision.HIGHEST)
    planes = planes.reshape(depth, NA_HEADS, n_dr, GRID_W, GRID_W)
    planes = jnp.where(col_ok, planes, MASK_NEG) * LOG2E
    planes = jnp.transpose(planes, (0, 1, 3, 2, 4))
    cases = []
    for rel0, *offs in NA_CASES:
        members = []
        for j in range(NA_PAIR):
            dr0 = offs[j] - rel0 - j + NA_KR - 1
            rows = jnp.pad(planes[:, :, :, dr0:dr0 + NA_KR], ((0, 0),) * 3 + ((offs[j], NA_WIN - NA_KR - offs[j]), (0, 0)),
                           constant_values=MASK_NEG * LOG2E)
            members.append(rows.reshape(depth, NA_HEADS, GRID_W, NA_WIN * GRID_W))
        cases.append(jnp.stack(members, axis=2))
    t = jnp.stack(cases, axis=1)
    return t.reshape(depth, len(NA_CASES), NA_HEADS, NA_PAIR * GRID_W, NA_WIN * GRID_W)


def neighborhood_attention(z, bias, seqs, q_col_block, dh):
    m = z.shape[0]
    hw = NA_HEADS * dh
    n_rows = m // GRID_W
    bounds = [(s0 // GRID_W, t // GRID_W) for s0, t in seqs]
    for r0, nr in bounds:
        assert nr % NA_PAIR == 0 and r0 % NA_PAIR == 0 and nr >= NA_KR + 4

    def geometry(g):
        r = g * NA_PAIR
        lo = jnp.int32(0)
        nr_ = jnp.int32(0)
        for r0, nr in bounds:
            inside = (r >= r0) & (r < r0 + nr)
            lo = jnp.where(inside, r0, lo)
            nr_ = jnp.where(inside, nr, nr_)
        top = r - lo
        bot = lo + nr_ - NA_PAIR - r
        u = jnp.clip(r - NA_KR // 2, lo, lo + nr_ - NA_WIN)
        case = jnp.where(top == 0, 0, jnp.where(top == 2, 1, jnp.where(bot == 0, 4, jnp.where(bot == 2, 3, 2))))
        return u, case

    tq, tk = NA_PAIR * GRID_W, NA_WIN * GRID_W

    def kv_spec(col_block):
        return pl.BlockSpec((pl.Element(tk), pl.Element(hw)),
                            lambda g: (geometry(g)[0] * GRID_W, col_block * hw))

    in_specs = [pl.BlockSpec((tq, hw), lambda g: (g, q_col_block)),
                kv_spec(q_col_block + 1), kv_spec(q_col_block + 2),
                pl.BlockSpec((1, NA_HEADS, tq, tk), lambda g: (geometry(g)[1], 0, 0, 0))]
    blk = (2 * _nbytes((tq, hw), z.dtype) + 2 * _nbytes((tk, hw), z.dtype)
           + 2 * _nbytes((NA_HEADS, tq, tk), jnp.float32))
    return pl.pallas_call(
        functools.partial(_na_kernel, dh=dh),
        out_shape=jax.ShapeDtypeStruct((m, hw), jnp.bfloat16),
        grid=(n_rows // NA_PAIR,),
        in_specs=in_specs,
        out_specs=pl.BlockSpec((tq, hw), lambda g: (g, 0)),
        compiler_params=_params(blk, ("parallel",)),
        name="neighborhood_attention",
    )(z, z, z, bias)


def _xattn_kernel(q_ref, k_ref, v_ref, o_ref, *, dh):
    scale = dh ** -0.5
    for h in range(X_HEADS):
        sl = slice(h * dh, (h + 1) * dh)
        s = lax.dot_general(q_ref[:, sl], k_ref[:, sl], (((1,), (1,)), ((), ())),
                            preferred_element_type=jnp.float32) * scale
        s = s - jnp.max(s, axis=1, keepdims=True)
        e = jnp.exp(s)
        p = e / jnp.sum(e, axis=1, keepdims=True)
        o_ref[:, sl] = jnp.dot(p.astype(v_ref.dtype), v_ref[:, sl],
                               preferred_element_type=jnp.float32).astype(o_ref.dtype)


def memory_attention(q, k, v, seqs, n_mem):
    m, xw = q.shape
    dh = xw // X_HEADS
    tm = XATTN_TILE
    for s0, t in seqs:
        tm = min(tm, _tile(t, tm))
    starts = [s0 // tm for s0, _ in seqs]
    for s0, _ in seqs:
        assert s0 % tm == 0

    def seq_of(i):
        s = jnp.int32(0)
        for idx, st in enumerate(starts):
            s = jnp.where(i >= st, idx, s)
        return s

    blk = 2 * _nbytes((tm, xw), q.dtype) + 2 * _nbytes((n_mem, xw), k.dtype) + 2 * _nbytes((tm, n_mem), jnp.float32)
    return pl.pallas_call(
        functools.partial(_xattn_kernel, dh=dh),
        out_shape=jax.ShapeDtypeStruct((m, xw), jnp.bfloat16),
        grid=(m // tm,),
        in_specs=[pl.BlockSpec((tm, xw), lambda i: (i, 0)),
                  pl.BlockSpec((n_mem, xw), lambda i: (seq_of(i), 0)),
                  pl.BlockSpec((n_mem, xw), lambda i: (seq_of(i), 0))],
        out_specs=pl.BlockSpec((tm, xw), lambda i: (i, 0)),
        compiler_params=_params(blk, ("parallel",)),
        name="memory_attention",
    )(q, k, v)


def kernel(x_prompt, x_sample, mem_prompt, mem_sample, w_in, w_out, g_pre_mix, g_post_mix, i_bias, f_bias, g_mlstm_out, rpb, g_pre_xattn, g_post_xattn, g_mem, wq_x, wk_x, wv_x, wo_x, g_pre_ffn, g_post_ffn, w_gate, w_up, w_down):
    bf16 = jnp.bfloat16
    depth = w_in.shape[0]
    d = x_prompt.shape[-1]
    bp, tp, _ = x_prompt.shape
    bs, ts, _ = x_sample.shape
    n_mem = mem_prompt.shape[1]
    mix_w = w_out.shape[1]
    v_w = mix_w // 2
    dv = v_w // M_HEADS
    dk = dv // 2
    qk_w = M_HEADS * dk
    na_dh = v_w // NA_HEADS
    n_gates = 4 * M_HEADS
    d_ff = w_gate.shape[2]
    assert w_in.shape[2] == 2 * qk_w + 2 * v_w + n_gates + 3 * v_w

    seqs = [(b * tp, tp) for b in range(bp)] + [(bp * tp + b * ts, ts) for b in range(bs)]
    mem =jnp.concatenate([mem_prompt.reshape(bp * n_mem, d), mem_sample.reshape(bs * n_mem, d)], axis=0)

    m_cols = 2 * qk_w + 2 * v_w
    w_main = jnp.concatenate([w_in[:, :, :m_cols], w_in[:, :, m_cols + n_gates:]], axis=2).astype(bf16)
    w_gates = jnp.pad(w_in[:, :, m_cols:m_cols + n_gates], ((0, 0), (0, 0), (0, GATE_LANES - n_gates))).astype(bf16)
    gate_bias = jnp.concatenate([i_bias[:, 0], f_bias[:, 0], i_bias[:, 1], f_bias[:, 1]], axis=1)
    gate_bias = jnp.pad(gate_bias, ((0, 0), (0, GATE_LANES - n_gates))).astype(jnp.float32)
    w_out_b = w_out.astype(bf16)
    wq, wk, wv, wo = (w.astype(bf16) for w in (wq_x, wk_x, wv_x, wo_x))
    wg, wu, wd = (w.astype(bf16) for w in (w_gate, w_up, w_down))
    na_bias = _na_bias_tables(rpb)

    tm = _tile(np.gcd(bp * tp, bs * ts), TOK_TILE)
    srcs, y, g_post = [x_prompt.reshape(bp * tp, d), x_sample.reshape(bs * ts, d)], None, None
    for l in range(depth):
        tm_in = _tile(tm, IN_PROJ_TM)
        h, z, gates = norm_ahead_proj(srcs, y, g_post, g_pre_mix[l], ((w_main, l, 0), (w_gates, l, 0)), "in_proj",
                                      tm_in, _col_tile(w_main.shape[2], IN_PROJ_TN), _tile(tm_in, IN_PROJ_CHUNKS))
        gb = gate_bias[l].reshape(1, GATE_LANES)
        h_fw = mlstm_scan(z, gates, gb, seqs, False, dk, dv)
        y_m = mlstm_scan(z, gates, gb, seqs, True, dk, dv, combine=(h_fw, g_mlstm_out[l]))
        y_n = neighborhood_attention(z, na_bias[l], seqs, m_cols // v_w, na_dh)
        tm_out = _tile(tm, OUT_PROJ_TM)
        tn_out = _col_tile(d, OUT_PROJ_TN)
        h, a = matmul_norm_behind([y_m, y_n], [(w_out_b, l, 0), (w_out_b, l, 1)], h, g_post_mix[l], g_pre_xattn[l],
                                  tm_out, tn_out, min(d // tn_out, OUT_PROJ_CHUNKS))
        q = matmul([a], [(wq, l, 0)], bf16)
        mem_n = rmsnorm_cast(mem, g_mem[l])
        k = matmul([mem_n], [(wk, l, 0)], bf16)
        v = matmul([mem_n], [(wv, l, 0)], bf16)
        o = memory_attention(q, k, v, seqs, n_mem)
        y = matmul([o], [(wo, l, 0)], jnp.float32)
        h, hidden = norm_ahead_proj([h], y, g_post_xattn[l], g_pre_ffn[l], ((wg, l, 0), (wu, l, 0)), "gate_up",
                                    tm, _tile(d_ff, FF_UP_TN), _tile(tm, FF_UP_CHUNKS))
        y = matmul([hidden], [(wd, l, 0)], jnp.float32, tn_pref=FF_DOWN_TILE, tm_pref=FF_DOWN_TILE)
        srcs, g_post = [h], g_post_ffn[l]

    out_p, out_s = residual_split(h, y, g_post, bp * tp)
    return (out_p.reshape(bp, tp, d), out_s.reshape(bs, ts, d))
```
